```python
import jax, jax.numpy as jnp
from jax import lax
import numpy as np

D_MODEL = 1024
BATCH = 32
SEQ = 2048
DEPTH = 1

CHUNK = 64
Q_BLOCK = 128
FOX_HEADS = 8
FOX_HEAD_DIM = D_MODEL // 16
FOX_WIDTH = FOX_HEADS * FOX_HEAD_DIM
MLSTM_HEADS = 4
MLSTM_HEAD_DIM = D_MODEL // 8
MLSTM_WIDTH = MLSTM_HEADS * MLSTM_HEAD_DIM
CONV_WIDTH = 4
D_FF = 128 * ((8 * D_MODEL // 3 + 127) // 128)
RMS_EPS = 1e-6
IN_COLS = 3 * FOX_WIDTH + FOX_HEADS + 4 * MLSTM_WIDTH + 2 * MLSTM_HEADS + 2 * D_MODEL

kernel_name = 'fox_mlstm_macaron_gated_hybrid'


def _split_points():
    sizes = [FOX_WIDTH, FOX_WIDTH, FOX_WIDTH, FOX_HEADS,
             2 * MLSTM_WIDTH, MLSTM_WIDTH, MLSTM_HEADS, MLSTM_HEADS, MLSTM_WIDTH]
    pts, acc = [], 0
    for s in sizes:
        acc += s
        pts.append(acc)
    return pts


def rms_norm(x, g):
    xf = x.astype(jnp.float32)
    y = xf * lax.rsqrt(jnp.mean(xf * xf, axis=-1, keepdims=True) + RMS_EPS)
    return (y * g.astype(jnp.float32)).astype(x.dtype)


def swiglu_ffn(x, w_in, w_out):
    gate, up = jnp.split(x @ w_in, 2, axis=-1)
    return (jax.nn.silu(gate) * up) @ w_out


def causal_depthwise_conv(x, w, b):
    K, C = w.shape
    y = lax.conv_general_dilated(x, w[:, None, :].astype(x.dtype), window_strides=(1,),
                                 padding=[(K - 1, 0)], dimension_numbers=('NWC', 'WIO', 'NWC'),
                                 feature_group_count=C)
    return y + b.astype(x.dtype)


def forgetting_attention(q, k, v, log_f):
    B, S, H, Dh = q.shape
    c = jnp.cumsum(log_f, axis=1).transpose(0, 2, 1)
    scale = Dh ** -0.5
    outs = []
    for i in range(S // Q_BLOCK):
        q0, q1 = i * Q_BLOCK, (i + 1) * Q_BLOCK
        s = jnp.einsum('bqhd,bkhd->bhqk', q[:, q0:q1], k[:, :q1]) * scale
        s = s + c[:, :, q0:q1, None] - c[:, :, None, :q1]
        t_pos = jnp.arange(q0, q1)[:, None]
        s_pos = jnp.arange(q1)[None, :]
        s = jnp.where(s_pos <= t_pos, s, -jnp.inf)
        p = jax.nn.softmax(s, axis=-1)
        outs.append(jnp.einsum('bhqk,bkhd->bqhd', p, v[:, :q1]))
    return jnp.concatenate(outs, axis=1)


def mlstm_chunkwise(q, k, v, i_pre, log_f):
    B, S, H, Dk = q.shape
    Dv = v.shape[-1]
    nc = S // CHUNK

    def to_chunks(a):
        a = a.reshape((B, nc, CHUNK, H) + a.shape[3:])
        return jnp.moveaxis(a, (1, 3), (0, 2))

    xs = (to_chunks(q * Dk ** -0.5), to_chunks(k), to_chunks(v), to_chunks(i_pre), to_chunks(log_f))
    causal = jnp.tril(jnp.ones((CHUNK, CHUNK), dtype=bool))

    def step(carry, xs_c):
        C, n, m = carry
        qb, kb, vb, ib, fb = xs_c
        b = jnp.cumsum(fb, axis=-1)
        g = b[..., -1]
        dmat = jnp.where(causal, b[..., :, None] - b[..., None, :] + ib[..., None, :], -jnp.inf)
        m_inter = b + m[..., None]
        m_t = jnp.maximum(jnp.max(dmat, axis=-1), m_inter)
        s = jnp.einsum('bhld,bhsd->bhls', qb, kb) * jnp.exp(dmat - m_t[..., None])
        inter = jnp.exp(m_inter - m_t)
        num = jnp.einsum('bhls,bhse->bhle', s, vb) + inter[..., None] * jnp.einsum('bhld,bhde->bhle', qb, C)
        den = jnp.sum(s, axis=-1) + inter * jnp.einsum('bhld,bhd->bhl', qb, n)
        h = num / jnp.maximum(jnp.abs(den), jnp.exp(-m_t))[..., None]
        kdec = g[..., None] - b + ib
        m_new = jnp.maximum(g + m, jnp.max(kdec, axis=-1))
        wk = jnp.exp(kdec - m_new[..., None])
        carry_dec = jnp.exp(g + m - m_new)
        C_new = carry_dec[..., None, None] * C + jnp.einsum('bhs,bhsd,bhse->bhde', wk, kb, vb)
        n_new = carry_dec[..., None] * n + jnp.einsum('bhs,bhsd->bhd', wk, kb)
        return (C_new, n_new, m_new), h

    init = (jnp.zeros((B, H, Dk, Dv), jnp.float32), jnp.zeros((B, H, Dk), jnp.float32),
            jnp.zeros((B, H), jnp.float32))
    _, hs = lax.scan(step, init, xs)
    return jnp.moveaxis(hs, (0, 2), (1, 3)).reshape(B, S, H * Dv)


def hybrid_mixer(h, w_in, fox_f_bias, mlstm_conv_w, mlstm_conv_b, mlstm_i_bias, mlstm_f_bias,
                 w_up_fox, w_up_mlstm, w_out):
    B, S, _ = h.shape
    proj = (h @ w_in).astype(jnp.float32)
    fq, fk, fv, ff, mqk, mv, mi, mf, mo, gates = jnp.split(proj, _split_points(), axis=-1)
    fox_log_f = jax.nn.log_sigmoid(ff + fox_f_bias.astype(jnp.float32))
    a = forgetting_attention(fq.reshape(B, S, FOX_HEADS, FOX_HEAD_DIM),
                             fk.reshape(B, S, FOX_HEADS, FOX_HEAD_DIM),
                             fv.reshape(B, S, FOX_HEADS, FOX_HEAD_DIM), fox_log_f)
    a = a.reshape(B, S, FOX_WIDTH) @ w_up_fox.astype(jnp.float32)
    mqk = jax.nn.silu(causal_depthwise_conv(mqk, mlstm_conv_w.astype(jnp.float32), mlstm_conv_b))
    mq, mk = jnp.split(mqk, 2, axis=-1)
    hm = mlstm_chunkwise(mq.reshape(B, S, MLSTM_HEADS, MLSTM_HEAD_DIM),
                         mk.reshape(B, S, MLSTM_HEADS, MLSTM_HEAD_DIM),
                         mv.reshape(B, S, MLSTM_HEADS, MLSTM_HEAD_DIM),
                         mi + mlstm_i_bias.astype(jnp.float32),
                         jax.nn.log_sigmoid(mf + mlstm_f_bias.astype(jnp.float32)))
    bm = (jax.nn.sigmoid(mo) * hm) @ w_up_mlstm.astype(jnp.float32)
    g_a, g_b = jnp.split(gates, 2, axis=-1)
    merged = jax.nn.sigmoid(g_a) * a + jax.nn.sigmoid(g_b) * bm
    return (merged @ w_out.astype(jnp.float32)).astype(h.dtype)


def setup_inputs(seed: int = 0) -> dict:
    key = jax.random.key(seed)
    ks = jax.random.split(key, 20)
    L, D = DEPTH, D_MODEL

    def nrm(k, shape, fan_in):
        return jax.random.normal(k, shape, jnp.float32) * fan_in ** -0.5

    def gain(k, shape):
        return 1.0 + 0.05 * jax.random.normal(k, shape, jnp.float32)

    return {
        'x': jax.random.normal(ks[0], (BATCH, SEQ, D), jnp.float32),
        'ffn1_norm': gain(ks[1], (L, D)),
        'ffn1_w_in': nrm(ks[2], (L, D, 2 * D_FF), D),
        'ffn1_w_out': nrm(ks[3], (L, D_FF, D), D_FF),
        'mix_norm': gain(ks[4], (L, D)),
        'w_in': nrm(ks[5], (L, D, IN_COLS), D),
        'fox_f_bias': 1.0 + 0.1 * jax.random.normal(ks[6], (L, FOX_HEADS), jnp.float32),
        'mlstm_conv_w': nrm(ks[7], (L, CONV_WIDTH, 2 * MLSTM_WIDTH), CONV_WIDTH),
        'mlstm_conv_b': 0.02 * jax.random.normal(ks[8], (L, 2 * MLSTM_WIDTH), jnp.float32),
        'mlstm_i_bias': 0.1 * jax.random.normal(ks[9], (L, MLSTM_HEADS), jnp.float32),
        'mlstm_f_bias': jnp.linspace(3.0, 6.0, MLSTM_HEADS, dtype=jnp.float32)[None, :]
                        + 0.1 * jax.random.normal(ks[10], (L, MLSTM_HEADS), jnp.float32),
        'w_up_fox': nrm(ks[11], (L, FOX_WIDTH, D), FOX_WIDTH),
        'w_up_mlstm': nrm(ks[12], (L, MLSTM_WIDTH, D), MLSTM_WIDTH),
        'w_out': nrm(ks[13], (L, D, D), D),
        'ffn2_norm': gain(ks[14], (L, D)),
        'ffn2_w_in': nrm(ks[15], (L, D, 2 * D_FF), D),
        'ffn2_w_out': nrm(ks[16], (L, D_FF, D), D_FF),
        'final_norm': gain(ks[17], (D,)),
    }


def reference(x, ffn1_norm, ffn1_w_in, ffn1_w_out, mix_norm, w_in, fox_f_bias, mlstm_conv_w,
              mlstm_conv_b, mlstm_i_bias, mlstm_f_bias, w_up_fox, w_up_mlstm, w_out,
              ffn2_norm, ffn2_w_in, ffn2_w_out, final_norm):
    for l in range(DEPTH):
        x = x + 0.5 * swiglu_ffn(rms_norm(x, ffn1_norm[l]), ffn1_w_in[l], ffn1_w_out[l])
        x = x + hybrid_mixer(rms_norm(x, mix_norm[l]), w_in[l], fox_f_bias[l], mlstm_conv_w[l],
                             mlstm_conv_b[l], mlstm_i_bias[l], mlstm_f_bias[l],
                             w_up_fox[l], w_up_mlstm[l], w_out[l])
        x = x + 0.5 * swiglu_ffn(rms_norm(x, ffn2_norm[l]), ffn2_w_in[l], ffn2_w_out[l])
    return rms_norm(x, final_norm)
```

```python
import functools

import jax
import jax.numpy as jnp
from jax import lax
from jax.experimental import pallas as pl
from jax.experimental.pallas import tpu as pltpu

D_MODEL = 1024
FOX_HEADS = 8
FOX_HEAD_DIM = 64
FOX_WIDTH = FOX_HEADS * FOX_HEAD_DIM
MLSTM_HEADS = 4
MLSTM_HEAD_DIM = 128
MLSTM_WIDTH = MLSTM_HEADS * MLSTM_HEAD_DIM
CONV_WIDTH = 4
D_FF = 2816
RMS_EPS = 1e-6
N_SMALL = 16

V7X_VMEM_BYTES = 64 * 1024 * 1024
LANES = 128
SUBLANES = 8

ROW_TILE = 512
FF_CHUNK = 256
COL_CHUNK = 512
ATT_TQ = 256
ATT_TK = 512
MLSTM_CHUNK = 256
VMEM_LIMIT = 56 * 1024 * 1024

F32 = jnp.float32
BF16 = jnp.bfloat16


def _rms(x, g):
    ms = jnp.mean(x * x, axis=-1, keepdims=True)
    return x * lax.rsqrt(ms + RMS_EPS) * g


def _sigmoid(x):
    return 1.0 / (1.0 + jnp.exp(-x))


def _log_sigmoid(x):
    return jnp.minimum(x, 0.0) - jnp.log1p(jnp.exp(-jnp.abs(x)))


def _resident(shape):
    nd = len(shape)
    return pl.BlockSpec(shape, lambda *_: (0,) * nd, pipeline_mode=pl.Buffered(1))


def _cumsum_lanes(x):
    n = x.shape[-1]
    lane = lax.broadcasted_iota(jnp.int32, x.shape, x.ndim - 1)
    k = 1
    while k < n:
        x = x + jnp.where(lane >= k, pltpu.roll(x, k, axis=x.ndim - 1), 0.0)
        k *= 2
    return x


def _ffn_kernel(x_ref, g_ref, win_ref, wout_ref, fg_ref, o_ref, h_scr, a_scr, *, final_norm):
    x = x_ref[...]
    h_scr[...] = _rms(x, g_ref[...]).astype(BF16)
    for f in range(D_FF // FF_CHUNK):
        lo = f * FF_CHUNK
        h = h_scr[...]
        gate = jnp.dot(h, win_ref[:, lo:lo + FF_CHUNK], preferred_element_type=F32)
        up = jnp.dot(h, win_ref[:, D_FF + lo:D_FF + lo + FF_CHUNK], preferred_element_type=F32)
        a_scr[:, lo:lo + FF_CHUNK] = (gate * _sigmoid(gate) * up).astype(BF16)
    for c in range(D_MODEL // COL_CHUNK):
        lo = c * COL_CHUNK
        y = jnp.dot(a_scr[...], wout_ref[:, lo:lo + COL_CHUNK], preferred_element_type=F32)
        o_ref[:, lo:lo + COL_CHUNK] = x_ref[:, lo:lo + COL_CHUNK] + 0.5 * y
    if final_norm:
        o_ref[...] = _rms(o_ref[...], fg_ref[...])


def _ffn(x, g, w_in, w_out, fg, final_norm):
    t = x.shape[0]
    row = pl.BlockSpec((ROW_TILE, D_MODEL), lambda i: (i, 0))
    return pl.pallas_call(
        functools.partial(_ffn_kernel, final_norm=final_norm),
        grid=(t // ROW_TILE,),
        in_specs=[row, _resident((1, D_MODEL)), _resident(w_in.shape), _resident(w_out.shape),
                  _resident((1, D_MODEL))],
        out_specs=row,
        out_shape=jax.ShapeDtypeStruct((t, D_MODEL), F32),
        scratch_shapes=[pltpu.VMEM((ROW_TILE, D_MODEL), BF16), pltpu.VMEM((ROW_TILE, D_FF), BF16)],
        compiler_params=pltpu.CompilerParams(dimension_semantics=("arbitrary",),
                                             vmem_limit_bytes=VMEM_LIMIT),
        name="ffn_final" if final_norm else "ffn",
    )(x, g, w_in, w_out, fg)


def _inproj_kernel(x_ref, g_ref, wa_ref, wm_ref, wg_ref, ws_ref, bs_ref, cw_ref, cb_ref,
                   qkv_ref, mqk_ref, mv_ref, mo_ref, gates_ref, sg_ref, h_scr, xpad, *, tiles_per_seq):
    i = pl.program_id(0)
    h_scr[...] = _rms(x_ref[...], g_ref[...]).astype(BF16)

    @pl.when(i % tiles_per_seq == 0)
    def _():
        xpad[0:SUBLANES, :] = jnp.zeros((SUBLANES, 2 * MLSTM_WIDTH), F32)

    @pl.when(i % tiles_per_seq != 0)
    def _():
        xpad[0:SUBLANES, :] = xpad[ROW_TILE:ROW_TILE + SUBLANES, :]

    def proj(w_ref, lo):
        return jnp.dot(h_scr[...], w_ref[:, lo:lo + COL_CHUNK], preferred_element_type=F32)

    for c in range(3 * FOX_WIDTH // COL_CHUNK):
        lo = c * COL_CHUNK
        qkv_ref[:, lo:lo + COL_CHUNK] = proj(wa_ref, lo).astype(BF16)
    for c in range(2 * MLSTM_WIDTH // COL_CHUNK):
        lo = c * COL_CHUNK
        xpad[SUBLANES:SUBLANES + ROW_TILE, lo:lo + COL_CHUNK] = proj(wm_ref, lo)
    mv_ref[...] = proj(wm_ref, 2 * MLSTM_WIDTH).astype(BF16)
    mo_ref[...] = proj(wm_ref, 3 * MLSTM_WIDTH).astype(BF16)
    for c in range(2 * D_MODEL // COL_CHUNK):
        lo = c * COL_CHUNK
        gates_ref[:, lo:lo + COL_CHUNK] = proj(wg_ref, lo).astype(BF16)

    sg = lax.dot_general(ws_ref[...], h_scr[...], (((1,), (1,)), ((), ())),
                         preferred_element_type=F32) + bs_ref[...]
    r = lax.broadcasted_iota(jnp.int32, sg.shape, 0)
    is_i_gate = (r >= FOX_HEADS) & (r < FOX_HEADS + MLSTM_HEADS)
    sg_ref[...] = jnp.where(is_i_gate, sg, _log_sigmoid(sg))

    qscale = MLSTM_HEAD_DIM ** -0.5
    for c in range(2 * MLSTM_WIDTH // LANES):
        lo = c * LANES
        y = cb_ref[:, lo:lo + LANES]
        for j in range(CONV_WIDTH):
            off = SUBLANES - (CONV_WIDTH - 1) + j
            y = y + cw_ref[j:j + 1, lo:lo + LANES] * xpad[off:off + ROW_TILE, lo:lo + LANES]
        y = y * _sigmoid(y)
        if lo < MLSTM_WIDTH:
            y = y * qscale
        mqk_ref[:, lo:lo + LANES] = y.astype(BF16)


def _inproj(x, g, wa, wm, wg, ws, bs, cw, cb, seq):
    t = x.shape[0]
    nt = t // ROW_TILE

    def row(w):
        return pl.BlockSpec((ROW_TILE, w), lambda i: (i, 0))

    out_shape = (
        jax.ShapeDtypeStruct((t, 3 * FOX_WIDTH), BF16),
        jax.ShapeDtypeStruct((t, 2 * MLSTM_WIDTH), BF16),
        jax.ShapeDtypeStruct((t, MLSTM_WIDTH), BF16),
        jax.ShapeDtypeStruct((t, MLSTM_WIDTH), BF16),
        jax.ShapeDtypeStruct((t, 2 * D_MODEL), BF16),
        jax.ShapeDtypeStruct((N_SMALL, t), F32),
    )
    return pl.pallas_call(
        functools.partial(_inproj_kernel, tiles_per_seq=seq // ROW_TILE),
        grid=(nt,),
        in_specs=[row(D_MODEL), _resident((1, D_MODEL)), _resident(wa.shape), _resident(wm.shape),
                  _resident(wg.shape), _resident(ws.shape), _resident(bs.shape), _resident(cw.shape),
                  _resident(cb.shape)],
        out_specs=(row(3 * FOX_WIDTH), row(2 * MLSTM_WIDTH), row(MLSTM_WIDTH), row(MLSTM_WIDTH),
                   row(2 * D_MODEL), pl.BlockSpec((N_SMALL, ROW_TILE), lambda i: (0, i))),
        out_shape=out_shape,
        scratch_shapes=[pltpu.VMEM((ROW_TILE, D_MODEL), BF16),
                        pltpu.VMEM((ROW_TILE + SUBLANES, 2 * MLSTM_WIDTH), F32)],
        compiler_params=pltpu.CompilerParams(dimension_semantics=("arbitrary",),
                                             vmem_limit_bytes=VMEM_LIMIT),
        name="inproj",
    )(x, g, wa, wm, wg, ws, bs, cw, cb)


def _fox_kernel(q_ref, k_ref, v_ref, lf_ref, o_ref, nb_scr):
    hp = pl.program_id(1)
    i = pl.program_id(2)
    nb_scr[...] = -_cumsum_lanes(lf_ref[0:FOX_HEADS, :])
    n_full = (i * ATT_TQ) // ATT_TK
    t_pos = i * ATT_TQ + lax.broadcasted_iota(jnp.int32, (ATT_TQ, ATT_TK), 0)
    s_off = lax.broadcasted_iota(jnp.int32, (ATT_TQ, ATT_TK), 1)

    for hh in range(2):
        cl = hh * FOX_HEAD_DIM
        q = q_ref[0, :, cl:cl + FOX_HEAD_DIM]
        head = 2 * hp + hh

        def scores(j):
            k = k_ref[0, pl.ds(j * ATT_TK, ATT_TK), cl:cl + FOX_HEAD_DIM]
            s = lax.dot_general(q, k, (((1,), (1,)), ((), ())), preferred_element_type=F32)
            return s + nb_scr[pl.ds(head, 1), pl.ds(j * ATT_TK, ATT_TK)]

        def update(j, s, carry):
            m, l, acc = carry
            m_new = jnp.maximum(m, jnp.max(s, axis=-1, keepdims=True))
            alpha = jnp.exp(m - m_new)
            p = jnp.exp(s - m_new)
            v = v_ref[0, pl.ds(j * ATT_TK, ATT_TK), cl:cl + FOX_HEAD_DIM]
            pv = jnp.dot(p.astype(BF16), v, preferred_element_type=F32)
            return m_new, alpha * l + jnp.sum(p, axis=-1, keepdims=True), alpha * acc + pv

        def body(j, carry):
            return update(j, scores(j), carry)

        init = (jnp.full((ATT_TQ, 1), -1e30, F32), jnp.zeros((ATT_TQ, 1), F32),
                jnp.zeros((ATT_TQ, FOX_HEAD_DIM), F32))
        carry = lax.fori_loop(0, n_full, body, init)
        s = jnp.where(n_full * ATT_TK + s_off <= t_pos, scores(n_full), -jnp.inf)
        m, l, acc = update(n_full, s, carry)
        o_ref[0, :, cl:cl + FOX_HEAD_DIM] = (acc / l).astype(BF16)


def _fox_attention(qkv, sg, batch, seq):
    qkv3 = qkv.reshape(batch, seq, 3 * FOX_WIDTH)
    pairs = FOX_HEADS // 2
    return pl.pallas_call(
        _fox_kernel,
        grid=(batch, pairs, seq // ATT_TQ),
        in_specs=[pl.BlockSpec((1, ATT_TQ, LANES), lambda b, p, i: (b, i, p)),
                  pl.BlockSpec((1, seq, LANES), lambda b, p, i: (b, 0, pairs + p)),
                  pl.BlockSpec((1, seq, LANES), lambda b, p, i: (b, 0, 2 * pairs + p)),
                  pl.BlockSpec((N_SMALL, seq), lambda b, p, i: (0, b))],
        out_specs=pl.BlockSpec((1, ATT_TQ, LANES), lambda b, p, i: (b, i, p)),
        out_shape=jax.ShapeDtypeStruct((batch, seq, FOX_WIDTH), BF16),
        scratch_shapes=[pltpu.VMEM((FOX_HEADS, seq), F32)],
        compiler_params=pltpu.CompilerParams(
            dimension_semantics=("arbitrary", "arbitrary", "arbitrary"), vmem_limit_bytes=VMEM_LIMIT),
        name="fox_attn",
    )(qkv3, qkv3, qkv3, sg)


def _mlstm_kernel(q_ref, k_ref, v_ref, mo_ref, sg_ref, o_ref, *, seq):
    hd = pl.program_id(1)
    lc = MLSTM_CHUNK
    t_idx = lax.broadcasted_iota(jnp.int32, (lc, lc), 0)
    s_idx = lax.broadcasted_iota(jnp.int32, (lc, lc), 1)
    causal = s_idx <= t_idx

    def chunk(c, carry):
        cmat, n_row, m_prev = carry
        r0 = pl.multiple_of(c * lc, lc)
        q = q_ref[0, pl.ds(r0, lc), :]
        k = k_ref[0, pl.ds(r0, lc), :]
        v = v_ref[0, pl.ds(r0, lc), :]
        i_row = sg_ref[pl.ds(FOX_HEADS + hd, 1), pl.ds(r0, lc)]
        lf_row = sg_ref[pl.ds(FOX_HEADS + MLSTM_HEADS + hd, 1), pl.ds(r0, lc)]
        lf8 = jnp.broadcast_to(lf_row, (SUBLANES, lc))
        b_row = _cumsum_lanes(lf8)[0:1, :]
        b_col = jnp.sum(jnp.where(causal, lf8[0:1, :], 0.0), axis=-1, keepdims=True)
        g = jnp.sum(lf_row, axis=-1, keepdims=True)
        a_row = i_row - b_row
        dmat = jnp.where(causal, b_col + a_row, -jnp.inf)
        m_inter = b_col + m_prev
        m_t = jnp.maximum(jnp.max(dmat, axis=-1, keepdims=True), m_inter)
        kt = k.astype(F32).T
        s = jnp.dot(q, kt.astype(BF16), preferred_element_type=F32) * jnp.exp(dmat - m_t)
        inter = jnp.exp(m_inter - m_t)
        num = (jnp.dot(s.astype(BF16), v, preferred_element_type=F32)
               + inter * jnp.dot(q, cmat.astype(BF16), preferred_element_type=F32))
        qn = jnp.sum(q.astype(F32) * n_row, axis=-1, keepdims=True)
        den = jnp.sum(s, axis=-1, keepdims=True) + inter * qn
        h = num / jnp.maximum(jnp.abs(den), jnp.exp(-m_t))
        o_ref[0, pl.ds(r0, lc), :] = (_sigmoid(mo_ref[0, pl.ds(r0, lc), :].astype(F32)) * h).astype(BF16)

        kdec = g + a_row
        m_new = jnp.maximum(g + m_prev, jnp.max(kdec, axis=-1, keepdims=True))
        wk = jnp.exp(kdec - m_new)
        cdec = jnp.exp(g + m_prev - m_new)
        kwt = (kt * wk).astype(BF16)
        cmat = cdec * cmat + jnp.dot(kwt, v, preferred_element_type=F32)
        wk8 = jnp.broadcast_to(wk, (SUBLANES, lc)).astype(BF16)
        n_row = cdec * n_row + jnp.dot(wk8, k, preferred_element_type=F32)[0:1, :]
        return cmat, n_row, m_new

    init = (jnp.zeros((MLSTM_HEAD_DIM, MLSTM_HEAD_DIM), F32), jnp.zeros((1, MLSTM_HEAD_DIM), F32),
            jnp.zeros((1, 1), F32))
    lax.fori_loop(0, seq // lc, chunk, init)


def _mlstm(mqk, mv, mo, sg, batch, seq):
    mqk3 = mqk.reshape(batch, seq, 2 * MLSTM_WIDTH)
    mv3 = mv.reshape(batch, seq, MLSTM_WIDTH)
    mo3 = mo.reshape(batch, seq, MLSTM_WIDTH)

    def head(off):
        return pl.BlockSpec((1, seq, MLSTM_HEAD_DIM), lambda b, h: (b, 0, off + h))

    return pl.pallas_call(
        functools.partial(_mlstm_kernel, seq=seq),
        grid=(batch, MLSTM_HEADS),
        in_specs=[head(0), head(MLSTM_HEADS), head(0), head(0),
                  pl.BlockSpec((N_SMALL, seq), lambda b, h: (0, b))],
        out_specs=head(0),
        out_shape=jax.ShapeDtypeStruct((batch, seq, MLSTM_WIDTH), BF16),
        compiler_params=pltpu.CompilerParams(dimension_semantics=("arbitrary", "arbitrary"),
                                             vmem_limit_bytes=VMEM_LIMIT),
        name="mlstm",
    )(mqk3, mqk3, mv3, mo3, sg)


def _merge_kernel(x_ref, a_ref, hm_ref, gates_ref, wuf_ref, wum_ref, wo_ref, o_ref, mg_scr):
    for c in range(D_MODEL // COL_CHUNK):
        lo = c * COL_CHUNK
        a = jnp.dot(a_ref[...], wuf_ref[:, lo:lo + COL_CHUNK], preferred_element_type=F32)
        bm = jnp.dot(hm_ref[...], wum_ref[:, lo:lo + COL_CHUNK], preferred_element_type=F32)
        ga = gates_ref[:, lo:lo + COL_CHUNK].astype(F32)
        gb = gates_ref[:, D_MODEL + lo:D_MODEL + lo + COL_CHUNK].astype(F32)
        mg_scr[:, lo:lo + COL_CHUNK] = (_sigmoid(ga) * a + _sigmoid(gb) * bm).astype(BF16)
    for c in range(D_MODEL // COL_CHUNK):
        lo = c * COL_CHUNK
        y = jnp.dot(mg_scr[...], wo_ref[:, lo:lo + COL_CHUNK], preferred_element_type=F32)
        o_ref[:, lo:lo + COL_CHUNK] = x_ref[:, lo:lo + COL_CHUNK] + y


def _merge(x, a, hm, gates, wuf, wum, wo):
    t = x.shape[0]

    def row(w):
        return pl.BlockSpec((ROW_TILE, w), lambda i: (i, 0))

    return pl.pallas_call(
        _merge_kernel,
        grid=(t // ROW_TILE,),
        in_specs=[row(D_MODEL), row(FOX_WIDTH), row(MLSTM_WIDTH), row(2 * D_MODEL),
                  _resident(wuf.shape), _resident(wum.shape), _resident(wo.shape)],
        out_specs=row(D_MODEL),
        out_shape=jax.ShapeDtypeStruct((t, D_MODEL), F32),
        scratch_shapes=[pltpu.VMEM((ROW_TILE, D_MODEL), BF16)],
        compiler_params=pltpu.CompilerParams(dimension_semantics=("arbitrary",),
                                             vmem_limit_bytes=VMEM_LIMIT),
        name="merge",
    )(x, a, hm, gates, wuf, wum, wo)


def kernel(x, ffn1_norm, ffn1_w_in, ffn1_w_out, mix_norm, w_in, fox_f_bias, mlstm_conv_w, mlstm_conv_b,
           mlstm_i_bias, mlstm_f_bias, w_up_fox, w_up_mlstm, w_out, ffn2_norm, ffn2_w_in, ffn2_w_out,
           final_norm):
    batch, seq, d = x.shape
    depth = ffn1_norm.shape[0]
    assert depth >= 1
    assert d == D_MODEL and seq % ROW_TILE == 0 and seq % ATT_TK == 0 and seq % MLSTM_CHUNK == 0
    xf = x.reshape(batch * seq, d)
    fg = final_norm.reshape(1, d)

    o_ff = 3 * FOX_WIDTH
    o_mqk = o_ff + FOX_HEADS
    o_mv = o_mqk + 2 * MLSTM_WIDTH
    o_mi = o_mv + MLSTM_WIDTH
    o_mf = o_mi + MLSTM_HEADS
    o_mo = o_mf + MLSTM_HEADS
    o_g = o_mo + MLSTM_WIDTH

    for l in range(depth):
        w = w_in[l]
        wa = jnp.concatenate([w[:, :FOX_WIDTH] * FOX_HEAD_DIM ** -0.5, w[:, FOX_WIDTH:o_ff]], axis=1).astype(BF16)
        wm = jnp.concatenate([w[:, o_mqk:o_mi], w[:, o_mo:o_g]], axis=1).astype(BF16)
        wg = w[:, o_g:].astype(BF16)
        ws = jnp.concatenate([w[:, o_ff:o_mqk], w[:, o_mi:o_mo]], axis=1).T.astype(BF16)
        bs = jnp.concatenate([fox_f_bias[l], mlstm_i_bias[l], mlstm_f_bias[l]]).reshape(N_SMALL, 1).astype(F32)

        xf = _ffn(xf, ffn1_norm[l].reshape(1, d), ffn1_w_in[l].astype(BF16), ffn1_w_out[l].astype(BF16), fg, False)
        qkv, mqk, mv, mo, gates, sg = _inproj(
            xf, mix_norm[l].reshape(1, d), wa, wm, wg, ws, bs, mlstm_conv_w[l].astype(F32),
            mlstm_conv_b[l].reshape(1, -1).astype(F32), seq)
        att = _fox_attention(qkv, sg, batch, seq).reshape(batch * seq, FOX_WIDTH)
        hm = _mlstm(mqk, mv, mo, sg, batch, seq).reshape(batch * seq, MLSTM_WIDTH)
        xf = _merge(xf, att, hm, gates, w_up_fox[l].astype(BF16), w_up_mlstm[l].astype(BF16),
                    w_out[l].astype(BF16))
        xf = _ffn(xf, ffn2_norm[l].reshape(1, d), ffn2_w_in[l].astype(BF16), ffn2_w_out[l].astype(BF16), fg,
                  final_norm=(l == depth - 1))
    return xf.reshape(batch, seq, d)
```

```python
import functools

import jax
import jax.numpy as jnp
from jax import lax
from jax.experimental import pallas as pl
from jax.experimental.pallas import tpu as pltpu

D_MODEL = 1024
FOX_HEADS = 8
FOX_HEAD_DIM = 64
FOX_WIDTH = FOX_HEADS * FOX_HEAD_DIM
MLSTM_HEADS = 4
MLSTM_HEAD_DIM = 128
MLSTM_WIDTH = MLSTM_HEADS * MLSTM_HEAD_DIM
CONV_WIDTH = 4
D_FF = 2816
RMS_EPS = 1e-6
N_SMALL = 16

V7X_VMEM_BYTES = 64 * 1024 * 1024
LANES = 128
SUBLANES = 8

ROW_TILE = 512
FF_CHUNK = 256
COL_CHUNK = 512
ATT_TQ = 256
ATT_TK = 256
MLSTM_CHUNK = 256
VMEM_LIMIT = 56 * 1024 * 1024

F32 = jnp.float32
BF16 = jnp.bfloat16


def _rms(x, g):
    ms = jnp.mean(x * x, axis=-1, keepdims=True)
    return x * lax.rsqrt(ms + RMS_EPS) * g


def _sigmoid(x):
    return 1.0 / (1.0 + jnp.exp(-x))


def _log_sigmoid(x):
    return jnp.minimum(x, 0.0) - jnp.log1p(jnp.exp(-jnp.abs(x)))


def _resident(shape):
    nd = len(shape)
    return pl.BlockSpec(shape, lambda *_: (0,) * nd, pipeline_mode=pl.Buffered(1))


def _cumsum_lanes(x):
    n = x.shape[-1]
    lane = lax.broadcasted_iota(jnp.int32, x.shape, x.ndim - 1)
    k = 1
    while k < n:
        x = x + jnp.where(lane >= k, pltpu.roll(x, k, axis=x.ndim - 1), 0.0)
        k *= 2
    return x


def _ffn_kernel(x_ref, g_ref, win_ref, wout_ref, fg_ref, o_ref, h_scr, a_scr, *, final_norm):
    x = x_ref[...]
    h_scr[...] = _rms(x, g_ref[...]).astype(BF16)
    for f in range(D_FF // FF_CHUNK):
        lo = f * FF_CHUNK
        h = h_scr[...]
        gate = jnp.dot(h, win_ref[:, lo:lo + FF_CHUNK], preferred_element_type=F32)
        up = jnp.dot(h, win_ref[:, D_FF + lo:D_FF + lo + FF_CHUNK], preferred_element_type=F32)
        a_scr[:, lo:lo + FF_CHUNK] = (gate * _sigmoid(gate) * up).astype(BF16)
    for c in range(D_MODEL // COL_CHUNK):
        lo = c * COL_CHUNK
        y = jnp.dot(a_scr[...], wout_ref[:, lo:lo + COL_CHUNK], preferred_element_type=F32)
        o_ref[:, lo:lo + COL_CHUNK] = x_ref[:, lo:lo + COL_CHUNK] + 0.5 * y
    if final_norm:
        o_ref[...] = _rms(o_ref[...], fg_ref[...])


def _ffn(x, g, w_in, w_out, fg, final_norm):
    t = x.shape[0]
    row = pl.BlockSpec((ROW_TILE, D_MODEL), lambda i: (i, 0))
    return pl.pallas_call(
        functools.partial(_ffn_kernel, final_norm=final_norm),
        grid=(t // ROW_TILE,),
        in_specs=[row, _resident((1, D_MODEL)), _resident(w_in.shape), _resident(w_out.shape),
                  _resident((1, D_MODEL))],
        out_specs=row,
        out_shape=jax.ShapeDtypeStruct((t, D_MODEL), F32),
        scratch_shapes=[pltpu.VMEM((ROW_TILE, D_MODEL), BF16), pltpu.VMEM((ROW_TILE, D_FF), BF16)],
        compiler_params=pltpu.CompilerParams(dimension_semantics=("arbitrary",),
                                             vmem_limit_bytes=VMEM_LIMIT),
        name="ffn_final" if final_norm else "ffn",
    )(x, g, w_in, w_out, fg)


def _inproj_kernel(x_ref, g_ref, wa_ref, wm_ref, wg_ref, ws_ref, bs_ref, cw_ref, cb_ref,
                   qkv_ref, mqk_ref, mv_ref, mo_ref, gates_ref, sg_ref, h_scr, xpad, *, tiles_per_seq):
    i = pl.program_id(0)
    h_scr[...] = _rms(x_ref[...], g_ref[...]).astype(BF16)

    @pl.when(i % tiles_per_seq == 0)
    def _():
        xpad[0:SUBLANES, :] = jnp.zeros((SUBLANES, 2 * MLSTM_WIDTH), F32)

    @pl.when(i % tiles_per_seq != 0)
    def _():
        xpad[0:SUBLANES, :] = xpad[ROW_TILE:ROW_TILE + SUBLANES, :]

    def proj(w_ref, lo):
        return jnp.dot(h_scr[...], w_ref[:, lo:lo + COL_CHUNK], preferred_element_type=F32)

    for c in range(3 * FOX_WIDTH // COL_CHUNK):
        lo = c * COL_CHUNK
        qkv_ref[:, lo:lo + COL_CHUNK] = proj(wa_ref, lo).astype(BF16)
    for c in range(2 * MLSTM_WIDTH // COL_CHUNK):
        lo = c * COL_CHUNK
        xpad[SUBLANES:SUBLANES + ROW_TILE, lo:lo + COL_CHUNK] = proj(wm_ref, lo)
    mv_ref[...] = proj(wm_ref, 2 * MLSTM_WIDTH).astype(BF16)
    mo_ref[...] = proj(wm_ref, 3 * MLSTM_WIDTH).astype(BF16)
    for c in range(2 * D_MODEL // COL_CHUNK):
        lo = c * COL_CHUNK
        gates_ref[:, lo:lo + COL_CHUNK] = proj(wg_ref, lo).astype(BF16)

    sg = lax.dot_general(ws_ref[...], h_scr[...], (((1,), (1,)), ((), ())),
                         preferred_element_type=F32) + bs_ref[...]
    r = lax.broadcasted_iota(jnp.int32, sg.shape, 0)
    is_i_gate = (r >= FOX_HEADS) & (r < FOX_HEADS + MLSTM_HEADS)
    sg_ref[...] = jnp.where(is_i_gate, sg, _log_sigmoid(sg))

    qscale = MLSTM_HEAD_DIM ** -0.5
    for c in range(2 * MLSTM_WIDTH // LANES):
        lo = c * LANES
        y = cb_ref[:, lo:lo + LANES]
        for j in range(CONV_WIDTH):
            off = SUBLANES - (CONV_WIDTH - 1) + j
            y = y + cw_ref[j:j + 1, lo:lo + LANES] * xpad[off:off + ROW_TILE, lo:lo + LANES]
        y = y * _sigmoid(y)
        if lo < MLSTM_WIDTH:
            y = y * qscale
        mqk_ref[:, lo:lo + LANES] = y.astype(BF16)


def _inproj(x, g, wa, wm, wg, ws, bs, cw, cb, seq):
    t = x.shape[0]
    nt = t // ROW_TILE

    def row(w):
        return pl.BlockSpec((ROW_TILE, w), lambda i: (i, 0))

    out_shape = (
        jax.ShapeDtypeStruct((t, 3 * FOX_WIDTH), BF16),
        jax.ShapeDtypeStruct((t, 2 * MLSTM_WIDTH), BF16),
        jax.ShapeDtypeStruct((t, MLSTM_WIDTH), BF16),
        jax.ShapeDtypeStruct((t, MLSTM_WIDTH), BF16),
        jax.ShapeDtypeStruct((t, 2 * D_MODEL), BF16),
        jax.ShapeDtypeStruct((N_SMALL, t), F32),
    )
    return pl.pallas_call(
        functools.partial(_inproj_kernel, tiles_per_seq=seq // ROW_TILE),
        grid=(nt,),
        in_specs=[row(D_MODEL), _resident((1, D_MODEL)), _resident(wa.shape), _resident(wm.shape),
                  _resident(wg.shape), _resident(ws.shape), _resident(bs.shape), _resident(cw.shape),
                  _resident(cb.shape)],
        out_specs=(row(3 * FOX_WIDTH), row(2 * MLSTM_WIDTH), row(MLSTM_WIDTH), row(MLSTM_WIDTH),
                   row(2 * D_MODEL), pl.BlockSpec((N_SMALL, ROW_TILE), lambda i: (0, i))),
        out_shape=out_shape,
        scratch_shapes=[pltpu.VMEM((ROW_TILE, D_MODEL), BF16),
                        pltpu.VMEM((ROW_TILE + SUBLANES, 2 * MLSTM_WIDTH), F32)],
        compiler_params=pltpu.CompilerParams(dimension_semantics=("arbitrary",),
                                             vmem_limit_bytes=VMEM_LIMIT),
        name="inproj",
    )(x, g, wa, wm, wg, ws, bs, cw, cb)


BIAS_PIECES = 3
VT_ROWS = FOX_HEAD_DIM + 16


def _fox_kernel(q_ref, k_ref, v_ref, lf_ref, o_ref, nb_scr, qa_scr, ka_scr, vt_scr, s_scr, p_scr, *, seq):
    hp = pl.program_id(1)
    tq = tk = ATT_TQ
    half = FOX_HEAD_DIM

    nb_scr[...] = -_cumsum_lanes(lf_ref[0:FOX_HEADS, :])
    row = lax.broadcasted_iota(jnp.int32, (SUBLANES, seq), 0)
    blocks = []
    for hh in range(2):
        rem = nb_scr[pl.ds(2 * hp + hh, 1), :]
        blk = jnp.zeros((SUBLANES, seq), F32)
        for p in range(BIAS_PIECES):
            piece = rem.astype(BF16).astype(F32)
            blk = jnp.where(row == p, piece, blk)
            rem = rem - piece
        blocks.append(blk)
    pad = jnp.zeros((half - SUBLANES, seq), F32)
    pieces = jnp.concatenate([blocks[1], pad, blocks[0], pad], axis=0)
    lane = lax.broadcasted_iota(jnp.int32, (tk, LANES), 1)

    def lane_set(lo, hi):
        return jnp.where((lane >= lo) & (lane < hi), 1.0, 0.0).astype(BF16)

    own = (lane_set(0, half), lane_set(half, 2 * half))
    bias_lanes = (lane_set(half, half + BIAS_PIECES), lane_set(0, BIAS_PIECES))
    for c in range(seq // tk):
        sl = slice(c * tk, (c + 1) * tk)
        bias_t = pieces[:, sl].T.astype(BF16)
        k = k_ref[0, sl, :]
        q = q_ref[0, sl, :]
        for hh in range(2):
            ka_scr[hh, sl, :] = k * (1.0 - bias_lanes[hh]) + bias_t * bias_lanes[hh]
            qa_scr[hh, sl, :] = q * own[hh] + bias_lanes[hh]
        vt = v_ref[0, sl, :].astype(F32).T
        vt_scr[0, 0:half, sl] = vt[0:half].astype(BF16)
        vt_scr[1, 0:half, sl] = vt[half:2 * half].astype(BF16)
    ones_rows = jnp.where(lax.broadcasted_iota(jnp.int32, (VT_ROWS - half, seq), 0) == 0, 1.0, 0.0).astype(BF16)
    vt_scr[0, half:VT_ROWS, :] = ones_rows
    vt_scr[1, half:VT_ROWS, :] = ones_rows

    nt = (((1,), (1,)), ((), ()))
    causal = lax.broadcasted_iota(jnp.int32, (tq, tq), 0) <= lax.broadcasted_iota(jnp.int32, (tq, tq), 1)
    for i in range(seq // tq):
        lo, n = i * tq, (i + 1) * tq
        outs = []
        for hh in range(2):
            qa = qa_scr[hh, lo:n, :]
            if i > 0:
                s_scr[hh, 0:lo, :] = lax.dot_general(ka_scr[hh, 0:lo, :], qa, nt, preferred_element_type=F32)
            diag = lax.dot_general(ka_scr[hh, lo:n, :], qa, nt, preferred_element_type=F32)
            s_scr[hh, lo:n, :] = jnp.where(causal, diag, -jnp.inf)
            m = jnp.max(s_scr[hh, 0:n, :], axis=0, keepdims=True)
            p_scr[hh, 0:n, :] = jnp.exp(s_scr[hh, 0:n, :] - m).astype(BF16)
            acc = jnp.dot(vt_scr[hh, :, 0:n], p_scr[hh, 0:n, :], preferred_element_type=F32)
            outs.append(acc[0:half] * (1.0 / acc[half:half + 1]))
        o_ref[0, lo:n, :] = jnp.concatenate(outs, axis=0).T.astype(BF16)


def _fox_attention(qkv, sg, batch, seq):
    qkv3 = qkv.reshape(batch, seq, 3 * FOX_WIDTH)
    pairs = FOX_HEADS // 2

    def blk(off):
        return pl.BlockSpec((1, seq, LANES), lambda b, p: (b, 0, off + p))

    return pl.pallas_call(
        functools.partial(_fox_kernel, seq=seq),
        grid=(batch, pairs),
        in_specs=[blk(0), blk(pairs), blk(2 * pairs), pl.BlockSpec((N_SMALL, seq), lambda b, p: (0, b))],
        out_specs=blk(0),
        out_shape=jax.ShapeDtypeStruct((batch, seq, FOX_WIDTH), BF16),
        scratch_shapes=[pltpu.VMEM((FOX_HEADS, seq), F32),
                        pltpu.VMEM((2, seq, LANES), BF16),
                        pltpu.VMEM((2, seq, LANES), BF16),
                        pltpu.VMEM((2, VT_ROWS, seq), BF16),
                        pltpu.VMEM((2, seq, ATT_TQ), F32),
                        pltpu.VMEM((2, seq, ATT_TQ), BF16)],
        compiler_params=pltpu.CompilerParams(dimension_semantics=("arbitrary", "arbitrary"),
                                             vmem_limit_bytes=VMEM_LIMIT),
        name="fox_attn",
    )(qkv3, qkv3, qkv3, sg)


def _mlstm_kernel(qk_ref, v_ref, mo_ref, sg_ref, o_ref, c_scr, *, seq):
    lc = MLSTM_CHUNK
    dk = MLSTM_HEAD_DIM
    t_idx = lax.broadcasted_iota(jnp.int32, (lc, lc), 0)
    s_idx = lax.broadcasted_iota(jnp.int32, (lc, lc), 1)
    causal = s_idx <= t_idx
    c_scr[...] = jnp.zeros(c_scr.shape, F32)

    def chunk(c, carry):
        r0 = pl.multiple_of(c * lc, lc)
        rows = pl.ds(r0, lc)
        new_carry = []
        for hd in range(MLSTM_HEADS):
            n_row, m_prev = carry[hd]
            cmat = c_scr[hd]
            q = qk_ref[0, rows, hd * dk:(hd + 1) * dk]
            k = qk_ref[0, rows, MLSTM_WIDTH + hd * dk:MLSTM_WIDTH + (hd + 1) * dk]
            v = v_ref[0, rows, hd * dk:(hd + 1) * dk]
            i_row = sg_ref[FOX_HEADS + hd:FOX_HEADS + hd + 1, rows]
            lf_row = sg_ref[FOX_HEADS + MLSTM_HEADS + hd:FOX_HEADS + MLSTM_HEADS + hd + 1, rows]
            lf8 = jnp.broadcast_to(lf_row, (SUBLANES, lc))
            b_row = _cumsum_lanes(lf8)[0:1, :]
            b_col = jnp.sum(jnp.where(causal, lf8[0:1, :], 0.0), axis=-1, keepdims=True)
            g = jnp.sum(lf_row, axis=-1, keepdims=True)
            a_row = i_row - b_row
            dmat = jnp.where(causal, b_col + a_row, -jnp.inf)
            m_inter = b_col + m_prev
            m_t = jnp.maximum(jnp.max(dmat, axis=-1, keepdims=True), m_inter)
            kt = k.astype(F32).T
            s = jnp.dot(q, kt.astype(BF16), preferred_element_type=F32) * jnp.exp(dmat - m_t)
            inter = jnp.exp(m_inter - m_t)
            num = (jnp.dot(s.astype(BF16), v, preferred_element_type=F32)
                   + inter * jnp.dot(q, cmat.astype(BF16), preferred_element_type=F32))
            qn = jnp.sum(q.astype(F32) * n_row, axis=-1, keepdims=True)
            den = jnp.sum(s, axis=-1, keepdims=True) + inter * qn
            h = num * (1.0 / jnp.maximum(jnp.abs(den), jnp.exp(-m_t)))
            gate = _sigmoid(mo_ref[0, rows, hd * dk:(hd + 1) * dk].astype(F32))
            o_ref[0, rows, hd * dk:(hd + 1) * dk] = (gate * h).astype(BF16)

            kdec = g + a_row
            m_new = jnp.maximum(g + m_prev, jnp.max(kdec, axis=-1, keepdims=True))
            wk = jnp.exp(kdec - m_new)
            cdec = jnp.exp(g + m_prev - m_new)
            kwt = (kt * wk).astype(BF16)
            c_scr[hd] = cdec * cmat + jnp.dot(kwt, v, preferred_element_type=F32)
            wk8 = jnp.broadcast_to(wk, (SUBLANES, lc)).astype(BF16)
            n_new = cdec * n_row + jnp.dot(wk8, k, preferred_element_type=F32)[0:1, :]
            new_carry.append((n_new, m_new))
        return tuple(new_carry)

    init = tuple((jnp.zeros((1, dk), F32), jnp.zeros((1, 1), F32)) for _ in range(MLSTM_HEADS))
    lax.fori_loop(0, seq // lc, chunk, init)


def _mlstm(mqk, mv, mo, sg, batch, seq):
    mqk3 = mqk.reshape(batch, seq, 2 * MLSTM_WIDTH)
    mv3 = mv.reshape(batch, seq, MLSTM_WIDTH)
    mo3 = mo.reshape(batch, seq, MLSTM_WIDTH)

    def seq_blk(w):
        return pl.BlockSpec((1, seq, w), lambda b: (b, 0, 0))

    return pl.pallas_call(
        functools.partial(_mlstm_kernel, seq=seq),
        grid=(batch,),
        in_specs=[seq_blk(2 * MLSTM_WIDTH), seq_blk(MLSTM_WIDTH), seq_blk(MLSTM_WIDTH),
                  pl.BlockSpec((N_SMALL, seq), lambda b: (0, b))],
        out_specs=seq_blk(MLSTM_WIDTH),
        out_shape=jax.ShapeDtypeStruct((batch, seq, MLSTM_WIDTH), BF16),
        scratch_shapes=[pltpu.VMEM((MLSTM_HEADS, MLSTM_HEAD_DIM, MLSTM_HEAD_DIM), F32)],
        compiler_params=pltpu.CompilerParams(dimension_semantics=("arbitrary",),
                                             vmem_limit_bytes=VMEM_LIMIT),
        name="mlstm",
    )(mqk3, mv3, mo3, sg)


def _merge_kernel(x_ref, a_ref, hm_ref, gates_ref, wuf_ref, wum_ref, wo_ref, o_ref, mg_scr):
    for c in range(D_MODEL // COL_CHUNK):
        lo = c * COL_CHUNK
        a = jnp.dot(a_ref[...], wuf_ref[:, lo:lo + COL_CHUNK], preferred_element_type=F32)
        bm = jnp.dot(hm_ref[...], wum_ref[:, lo:lo + COL_CHUNK], preferred_element_type=F32)
        ga = gates_ref[:, lo:lo + COL_CHUNK].astype(F32)
        gb = gates_ref[:, D_MODEL + lo:D_MODEL + lo + COL_CHUNK].astype(F32)
        mg_scr[:, lo:lo + COL_CHUNK] = (_sigmoid(ga) * a + _sigmoid(gb) * bm).astype(BF16)
    for c in range(D_MODEL // COL_CHUNK):
        lo = c * COL_CHUNK
        y = jnp.dot(mg_scr[...], wo_ref[:, lo:lo + COL_CHUNK], preferred_element_type=F32)
        o_ref[:, lo:lo + COL_CHUNK] = x_ref[:, lo:lo + COL_CHUNK] + y


def _merge(x, a, hm, gates, wuf, wum, wo):
    t = x.shape[0]

    def row(w):
        return pl.BlockSpec((ROW_TILE, w), lambda i: (i, 0))

    return pl.pallas_call(
        _merge_kernel,
        grid=(t // ROW_TILE,),
        in_specs=[row(D_MODEL), row(FOX_WIDTH), row(MLSTM_WIDTH), row(2 * D_MODEL),
                  _resident(wuf.shape), _resident(wum.shape), _resident(wo.shape)],
        out_specs=row(D_MODEL),
        out_shape=jax.ShapeDtypeStruct((t, D_MODEL), F32),
        scratch_shapes=[pltpu.VMEM((ROW_TILE, D_MODEL), BF16)],
        compiler_params=pltpu.CompilerParams(dimension_semantics=("arbitrary",),
                                             vmem_limit_bytes=VMEM_LIMIT),
        name="merge",
    )(x, a, hm, gates, wuf, wum, wo)


def kernel(x, ffn1_norm, ffn1_w_in, ffn1_w_out, mix_norm, w_in, fox_f_bias, mlstm_conv_w, mlstm_conv_b,
           mlstm_i_bias, mlstm_f_bias, w_up_fox, w_up_mlstm, w_out, ffn2_norm, ffn2_w_in, ffn2_w_out,
           final_norm):
    batch, seq, d = x.shape
    depth = ffn1_norm.shape[0]
    assert depth >= 1
    assert d == D_MODEL and seq % ROW_TILE == 0 and seq % ATT_TK == 0 and seq % MLSTM_CHUNK == 0
    xf = x.reshape(batch * seq, d)
    fg = final_norm.reshape(1, d)

    o_ff = 3 * FOX_WIDTH
    o_mqk = o_ff + FOX_HEADS
    o_mv = o_mqk + 2 * MLSTM_WIDTH
    o_mi = o_mv + MLSTM_WIDTH
    o_mf = o_mi + MLSTM_HEADS
    o_mo = o_mf + MLSTM_HEADS
    o_g = o_mo + MLSTM_WIDTH

    for l in range(depth):
        w = w_in[l]
        wa = jnp.concatenate([w[:, :FOX_WIDTH] * FOX_HEAD_DIM ** -0.5, w[:, FOX_WIDTH:o_ff]], axis=1).astype(BF16)
        wm = jnp.concatenate([w[:, o_mqk:o_mi], w[:, o_mo:o_g]], axis=1).astype(BF16)
        wg = w[:, o_g:].astype(BF16)
        ws = jnp.concatenate([w[:, o_ff:o_mqk], w[:, o_mi:o_mo]], axis=1).T.astype(BF16)
        bs = jnp.concatenate([fox_f_bias[l], mlstm_i_bias[l], mlstm_f_bias[l]]).reshape(N_SMALL, 1).astype(F32)

        xf = _ffn(xf, ffn1_norm[l].reshape(1, d), ffn1_w_in[l].astype(BF16), ffn1_w_out[l].astype(BF16), fg, False)
        qkv, mqk, mv, mo, gates, sg = _inproj(
            xf, mix_norm[l].reshape(1, d), wa, wm, wg, ws, bs, mlstm_conv_w[l].astype(F32),
            mlstm_conv_b[l].reshape(1, -1).astype(F32), seq)
        att = _fox_attention(qkv, sg, batch, seq).reshape(batch * seq, FOX_WIDTH)
        hm = _mlstm(mqk, mv, mo, sg, batch, seq).reshape(batch * seq, MLSTM_WIDTH)
        xf = _merge(xf, att, hm, gates, w_up_fox[l].astype(BF16), w_up_mlstm[l].astype(BF16),
                    w_out[l].astype(BF16))
        xf = _ffn(xf, ffn2_norm[l].reshape(1, d), ffn2_w_in[l].astype(BF16), ffn2_w_out[l].astype(BF16), fg,
                  final_norm=(l == depth - 1))
    return xf.reshape(batch, seq, d)
```

```python
import functools

import jax
import jax.numpy as jnp
from jax import lax
from jax.experimental import pallas as pl
from jax.experimental.pallas import tpu as pltpu

D_MODEL = 1024
FOX_HEADS = 8
FOX_HEAD_DIM = 64
FOX_WIDTH = FOX_HEADS * FOX_HEAD_DIM
MLSTM_HEADS = 4
MLSTM_HEAD_DIM = 128
MLSTM_WIDTH = MLSTM_HEADS * MLSTM_HEAD_DIM
CONV_WIDTH = 4
D_FF = 2816
RMS_EPS = 1e-6
N_SMALL = 16
SG_ROWS = 24
SG_I = FOX_HEADS
SG_LOGF = FOX_HEADS + MLSTM_HEADS
SG_B = N_SMALL + MLSTM_HEADS

V7X_VMEM_BYTES = 64 * 1024 * 1024
LANES = 128
SUBLANES = 8

ROW_TILE = 512
FF_CHUNK = 256
COL_CHUNK = 512
ATT_TQ = 512
ATT_TK = 256
MLSTM_CHUNK = 256
VMEM_LIMIT = 56 * 1024 * 1024

F32 = jnp.float32
BF16 = jnp.bfloat16


def _rms(x, g):
    ms = jnp.mean(x * x, axis=-1, keepdims=True)
    return x * lax.rsqrt(ms + RMS_EPS) * g


def _sigmoid(x):
    return 1.0 / (1.0 + jnp.exp(-x))


def _log_sigmoid(x):
    return jnp.minimum(x, 0.0) - jnp.log1p(jnp.exp(-jnp.abs(x)))


def _resident(shape):
    nd = len(shape)
    return pl.BlockSpec(shape, lambda *_: (0,) * nd, pipeline_mode=pl.Buffered(1))


def _cumsum_lanes(x, segment=None):
    n = x.shape[-1]
    segment = n if segment is None else segment
    pos = lax.broadcasted_iota(jnp.int32, x.shape, x.ndim - 1) & (segment - 1)
    k = 1
    while k < segment:
        x = x + jnp.where(pos >= k, pltpu.roll(x, k, axis=x.ndim - 1), 0.0)
        k *= 2
    return x


def _ffn_kernel(x_ref, g_ref, win_ref, wout_ref, fg_ref, o_ref, h_scr, a_scr, *, final_norm):
    x = x_ref[...]
    h_scr[...] = _rms(x, g_ref[...]).astype(BF16)
    for f in range(D_FF // FF_CHUNK):
        lo = f * FF_CHUNK
        h = h_scr[...]
        gate = jnp.dot(h, win_ref[:, lo:lo + FF_CHUNK], preferred_element_type=F32)
        up = jnp.dot(h, win_ref[:, D_FF + lo:D_FF + lo + FF_CHUNK], preferred_element_type=F32)
        a_scr[:, lo:lo + FF_CHUNK] = (gate * _sigmoid(gate) * up).astype(BF16)
    for c in range(D_MODEL // COL_CHUNK):
        lo = c * COL_CHUNK
        y = jnp.dot(a_scr[...], wout_ref[:, lo:lo + COL_CHUNK], preferred_element_type=F32)
        o_ref[:, lo:lo + COL_CHUNK] = x_ref[:, lo:lo + COL_CHUNK] + 0.5 * y
    if final_norm:
        o_ref[...] = _rms(o_ref[...], fg_ref[...])


def _ffn(x, g, w_in, w_out, fg, final_norm):
    t = x.shape[0]
    row = pl.BlockSpec((ROW_TILE, D_MODEL), lambda i: (i, 0))
    return pl.pallas_call(
        functools.partial(_ffn_kernel, final_norm=final_norm),
        grid=(t // ROW_TILE,),
        in_specs=[row, _resident((1, D_MODEL)), _resident(w_in.shape), _resident(w_out.shape),
                  _resident((1, D_MODEL))],
        out_specs=row,
        out_shape=jax.ShapeDtypeStruct((t, D_MODEL), F32),
        scratch_shapes=[pltpu.VMEM((ROW_TILE, D_MODEL), BF16), pltpu.VMEM((ROW_TILE, D_FF), BF16)],
        compiler_params=pltpu.CompilerParams(dimension_semantics=("arbitrary",),
                                             vmem_limit_bytes=VMEM_LIMIT),
        name="ffn_final" if final_norm else "ffn",
    )(x, g, w_in, w_out, fg)


def _inproj_kernel(x_ref, g_ref, wa_ref, wm_ref, wg_ref, ws_ref, bs_ref, cw_ref, cb_ref,
                   qkv_ref, mqk_ref, mv_ref, mo_ref, gates_ref, sg_ref, h_scr, xpad, ccar, *, tiles_per_seq):
    i = pl.program_id(0)
    h_scr[...] = _rms(x_ref[...], g_ref[...]).astype(BF16)

    @pl.when(i % tiles_per_seq == 0)
    def _():
        xpad[0:SUBLANES, :] = jnp.zeros((SUBLANES, 2 * MLSTM_WIDTH), F32)
        ccar[...] = jnp.zeros(ccar.shape, F32)

    @pl.when(i % tiles_per_seq != 0)
    def _():
        xpad[0:SUBLANES, :] = xpad[ROW_TILE:ROW_TILE + SUBLANES, :]

    sg = lax.dot_general(ws_ref[...], h_scr[...], (((1,), (1,)), ((), ())),
                         preferred_element_type=F32) + bs_ref[...]
    r = lax.broadcasted_iota(jnp.int32, sg.shape, 0)
    is_i_gate = (r >= FOX_HEADS) & (r < FOX_HEADS + MLSTM_HEADS)
    sg = jnp.where(is_i_gate, sg, _log_sigmoid(sg))
    c_fox = _cumsum_lanes(sg[0:FOX_HEADS]) + ccar[:, 0:1]
    ccar[...] = jnp.broadcast_to(c_fox[:, ROW_TILE - 1:ROW_TILE], ccar.shape)
    sg_ref[0:FOX_HEADS, :] = -c_fox
    sg_ref[FOX_HEADS:N_SMALL, :] = sg[FOX_HEADS:N_SMALL]
    sg_ref[N_SMALL:SG_ROWS, :] = _cumsum_lanes(sg[FOX_HEADS:N_SMALL], segment=MLSTM_CHUNK)

    def proj(w_ref, lo):
        return jnp.dot(h_scr[...], w_ref[:, lo:lo + COL_CHUNK], preferred_element_type=F32)

    for c in range(2 * MLSTM_WIDTH // COL_CHUNK):
        xpad[SUBLANES:SUBLANES + ROW_TILE, c * COL_CHUNK:(c + 1) * COL_CHUNK] = proj(wm_ref, c * COL_CHUNK)
    qscale = MLSTM_HEAD_DIM ** -0.5

    def conv_group(lo):
        y = cb_ref[:, lo:lo + LANES]
        for j in range(CONV_WIDTH):
            off = SUBLANES - (CONV_WIDTH - 1) + j
            y = y + cw_ref[j:j + 1, lo:lo + LANES] * xpad[off:off + ROW_TILE, lo:lo + LANES]
        y = y * _sigmoid(y)
        if lo < MLSTM_WIDTH:
            y = y * qscale
        mqk_ref[:, lo:lo + LANES] = y.astype(BF16)

    others = [(mv_ref, wm_ref, 2 * MLSTM_WIDTH, 0), (mo_ref, wm_ref, 3 * MLSTM_WIDTH, 0)]
    others += [(qkv_ref, wa_ref, c * COL_CHUNK, c * COL_CHUNK) for c in range(3 * FOX_WIDTH // COL_CHUNK)]
    others += [(gates_ref, wg_ref, c * COL_CHUNK, c * COL_CHUNK) for c in range(2 * D_MODEL // COL_CHUNK)]
    groups = list(range(0, 2 * MLSTM_WIDTH, LANES))
    for idx, (o_ref, w_ref, w_lo, o_lo) in enumerate(others):
        if idx < len(groups):
            conv_group(groups[idx])
        o_ref[:, o_lo:o_lo + COL_CHUNK] = proj(w_ref, w_lo).astype(BF16)
    for lo in groups[len(others):]:
        conv_group(lo)


def _inproj(x, g, wa, wm, wg, ws, bs, cw, cb, seq):
    t = x.shape[0]
    nt = t // ROW_TILE

    def row(w):
        return pl.BlockSpec((ROW_TILE, w), lambda i: (i, 0))

    out_shape = (
        jax.ShapeDtypeStruct((t, 3 * FOX_WIDTH), BF16),
        jax.ShapeDtypeStruct((t, 2 * MLSTM_WIDTH), BF16),
        jax.ShapeDtypeStruct((t, MLSTM_WIDTH), BF16),
        jax.ShapeDtypeStruct((t, MLSTM_WIDTH), BF16),
        jax.ShapeDtypeStruct((t, 2 * D_MODEL), BF16),
        jax.ShapeDtypeStruct((SG_ROWS, t), F32),
    )
    return pl.pallas_call(
        functools.partial(_inproj_kernel, tiles_per_seq=seq // ROW_TILE),
        grid=(nt,),
        in_specs=[row(D_MODEL), _resident((1, D_MODEL)), _resident(wa.shape), _resident(wm.shape),
                  _resident(wg.shape), _resident(ws.shape), _resident(bs.shape), _resident(cw.shape),
                  _resident(cb.shape)],
        out_specs=(row(3 * FOX_WIDTH), row(2 * MLSTM_WIDTH), row(MLSTM_WIDTH), row(MLSTM_WIDTH),
                   row(2 * D_MODEL), pl.BlockSpec((SG_ROWS, ROW_TILE), lambda i: (0, i))),
        out_shape=out_shape,
        scratch_shapes=[pltpu.VMEM((ROW_TILE, D_MODEL), BF16),
                        pltpu.VMEM((ROW_TILE + SUBLANES, 2 * MLSTM_WIDTH), F32),
                        pltpu.VMEM((FOX_HEADS, LANES), F32)],
        compiler_params=pltpu.CompilerParams(dimension_semantics=("arbitrary",),
                                             vmem_limit_bytes=VMEM_LIMIT),
        name="inproj",
    )(x, g, wa, wm, wg, ws, bs, cw, cb)


BIAS_PIECES = 3
VT_ROWS = FOX_HEAD_DIM + 16
MAX_SLAB = 64
ATT_STAGE = 256
PV_CHUNK = 512


def _fox_kernel(q_ref, k_ref, v_ref, sg_ref, o_ref, qa_scr, ka_scr, vt_scr, s_scr, p_scr, *, seq):
    hp = pl.program_id(1)
    tq = ATT_TQ
    tk = ATT_STAGE
    half = FOX_HEAD_DIM

    row = lax.broadcasted_iota(jnp.int32, (SUBLANES, seq), 0)
    blocks = []
    for hh in range(2):
        rem = sg_ref[pl.ds(2 * hp + hh, 1), :]
        blk = jnp.zeros((SUBLANES, seq), F32)
        for p in range(BIAS_PIECES):
            piece = rem.astype(BF16).astype(F32)
            blk = jnp.where(row == p, piece, blk)
            rem = rem - piece
        blocks.append(blk)
    pad = jnp.zeros((half - SUBLANES, seq), F32)
    pieces = jnp.concatenate([blocks[1], pad, blocks[0], pad], axis=0)
    lane = lax.broadcasted_iota(jnp.int32, (tk, LANES), 1)

    def lane_set(lo, hi):
        return jnp.where((lane >= lo) & (lane < hi), 1.0, 0.0).astype(BF16)

    own = (lane_set(0, half), lane_set(half, 2 * half))
    bias_lanes = (lane_set(half, half + BIAS_PIECES), lane_set(0, BIAS_PIECES))
    for c in range(seq // tk):
        sl = slice(c * tk, (c + 1) * tk)
        bias_t = pieces[:, sl].T.astype(BF16)
        k = k_ref[0, sl, :]
        q = q_ref[0, sl, :]
        for hh in range(2):
            ka_scr[hh, sl, :] = k * (1.0 - bias_lanes[hh]) + bias_t * bias_lanes[hh]
            qa_scr[hh, sl, :] = q * own[hh] + bias_lanes[hh]
        vt = v_ref[0, sl, :].astype(F32).T
        vt_scr[0, 0:half, sl] = vt[0:half].astype(BF16)
        vt_scr[1, 0:half, sl] = vt[half:2 * half].astype(BF16)
    ones_rows = jnp.where(lax.broadcasted_iota(jnp.int32, (VT_ROWS - half, seq), 0) == 0, 1.0, 0.0).astype(BF16)
    vt_scr[0, half:VT_ROWS, :] = ones_rows
    vt_scr[1, half:VT_ROWS, :] = ones_rows

    nt = (((1,), (1,)), ((), ()))
    hq = tq // 2
    tri = lax.broadcasted_iota(jnp.int32, (hq, hq), 0) <= lax.broadcasted_iota(jnp.int32, (hq, hq), 1)
    for i in range(seq // tq):
        lo, mid, n = i * tq, i * tq + hq, (i + 1) * tq
        outs = []
        for hh in range(2):
            qa = qa_scr[hh, lo:n, :]
            if lo > 0:
                s_scr[hh, 0:lo, :] = lax.dot_general(ka_scr[hh, 0:lo, :], qa, nt, preferred_element_type=F32)
            d1 = lax.dot_general(ka_scr[hh, lo:mid, :], qa, nt, preferred_element_type=F32)
            s_scr[hh, lo:mid, 0:hq] = jnp.where(tri, d1[:, 0:hq], -jnp.inf)
            s_scr[hh, lo:mid, hq:tq] = d1[:, hq:tq]
            d2 = lax.dot_general(ka_scr[hh, mid:n, :], qa[hq:tq], nt, preferred_element_type=F32)
            d2 = jnp.where(tri, d2, -jnp.inf)
            s_scr[hh, mid:n, hq:tq] = d2
            slab = jnp.max(s_scr[hh, 0:mid, :].reshape(mid // MAX_SLAB, MAX_SLAB, tq), axis=0)
            m = jnp.max(slab, axis=0, keepdims=True)
            m_right = jnp.maximum(m[:, hq:tq], jnp.max(d2, axis=0, keepdims=True))
            m = jnp.concatenate([m[:, 0:hq], m_right], axis=1)
            acc = None
            for c0 in range(0, mid, PV_CHUNK):
                c1 = min(c0 + PV_CHUNK, mid)
                p_scr[hh, c0:c1, :] = jnp.exp(s_scr[hh, c0:c1, :] - m).astype(BF16)
                part = jnp.dot(vt_scr[hh, :, c0:c1], p_scr[hh, c0:c1, :], preferred_element_type=F32)
                acc = part if acc is None else acc + part
            p_scr[hh, mid:n, hq:tq] = jnp.exp(s_scr[hh, mid:n, hq:tq] - m_right).astype(BF16)
            right = jnp.dot(vt_scr[hh, :, mid:n], p_scr[hh, mid:n, hq:tq], preferred_element_type=F32)
            acc = jnp.concatenate([acc[:, 0:hq], acc[:, hq:tq] + right], axis=1)
            outs.append(acc[0:half] * (1.0 / acc[half:half + 1]))
        o_ref[0, lo:n, :] = jnp.concatenate(outs, axis=0).T.astype(BF16)


def _fox_attention(qkv, sg, batch, seq):
    qkv3 = qkv.reshape(batch, seq, 3 * FOX_WIDTH)
    pairs = FOX_HEADS // 2

    def blk(off):
        return pl.BlockSpec((1, seq, LANES), lambda b, p: (b, 0, off + p))

    return pl.pallas_call(
        functools.partial(_fox_kernel, seq=seq),
        grid=(batch, pairs),
        in_specs=[blk(0), blk(pairs), blk(2 * pairs), pl.BlockSpec((SG_ROWS, seq), lambda b, p: (0, b))],
        out_specs=blk(0),
        out_shape=jax.ShapeDtypeStruct((batch, seq, FOX_WIDTH), BF16),
        scratch_shapes=[pltpu.VMEM((2, seq, LANES), BF16),
                        pltpu.VMEM((2, seq, LANES), BF16),
                        pltpu.VMEM((2, VT_ROWS, seq), BF16),
                        pltpu.VMEM((2, seq, ATT_TQ), F32),
                        pltpu.VMEM((2, seq, ATT_TQ), BF16)],
        compiler_params=pltpu.CompilerParams(dimension_semantics=("arbitrary", "arbitrary"),
                                             vmem_limit_bytes=VMEM_LIMIT),
        name="fox_attn",
    )(qkv3, qkv3, qkv3, sg)


def _mlstm_kernel(qk_ref, v_ref, mo_ref, sg_ref, o_ref, c_scr, *, seq):
    lc = MLSTM_CHUNK
    dk = MLSTM_HEAD_DIM
    t_idx = lax.broadcasted_iota(jnp.int32, (lc, lc), 0)
    s_idx = lax.broadcasted_iota(jnp.int32, (lc, lc), 1)
    causal = s_idx <= t_idx
    c_scr[...] = jnp.zeros(c_scr.shape, F32)

    def chunk(c, carry):
        r0 = pl.multiple_of(c * lc, lc)
        rows = pl.ds(r0, lc)
        new_carry = []
        for hd in range(MLSTM_HEADS):
            n_row, m_prev = carry[hd]
            cmat = c_scr[hd]
            q = qk_ref[0, rows, hd * dk:(hd + 1) * dk]
            k = qk_ref[0, rows, MLSTM_WIDTH + hd * dk:MLSTM_WIDTH + (hd + 1) * dk]
            v = v_ref[0, rows, hd * dk:(hd + 1) * dk]
            i_row = sg_ref[SG_I + hd:SG_I + hd + 1, rows]
            lf_row = sg_ref[SG_LOGF + hd:SG_LOGF + hd + 1, rows]
            b_row = sg_ref[SG_B + hd:SG_B + hd + 1, rows]
            b_col = jnp.sum(jnp.where(causal, lf_row, 0.0), axis=-1, keepdims=True)
            g = jnp.sum(lf_row, axis=-1, keepdims=True)
            a_row = i_row - b_row
            dmat = jnp.where(causal, b_col + a_row, -jnp.inf)
            m_inter = b_col + m_prev
            m_t = jnp.maximum(jnp.max(dmat, axis=-1, keepdims=True), m_inter)
            kt = k.astype(F32).T
            s = jnp.dot(q, kt.astype(BF16), preferred_element_type=F32) * jnp.exp(dmat - m_t)
            inter = jnp.exp(m_inter - m_t)
            num = (jnp.dot(s.astype(BF16), v, preferred_element_type=F32)
                   + inter * jnp.dot(q, cmat.astype(BF16), preferred_element_type=F32))
            qn = jnp.sum(q.astype(F32) * n_row, axis=-1, keepdims=True)
            den = jnp.sum(s, axis=-1, keepdims=True) + inter * qn
            h = num * (1.0 / jnp.maximum(jnp.abs(den), jnp.exp(-m_t)))
            gate = _sigmoid(mo_ref[0, rows, hd * dk:(hd + 1) * dk].astype(F32))
            o_ref[0, rows, hd * dk:(hd + 1) * dk] = (gate * h).astype(BF16)

            kdec = g + a_row
            m_new = jnp.maximum(g + m_prev, jnp.max(kdec, axis=-1, keepdims=True))
            wk = jnp.exp(kdec - m_new)
            cdec = jnp.exp(g + m_prev - m_new)
            kwt = (kt * wk).astype(BF16)
            c_scr[hd] = cdec * cmat + jnp.dot(kwt, v, preferred_element_type=F32)
            wk8 = jnp.broadcast_to(wk, (SUBLANES, lc)).astype(BF16)
            n_new = cdec * n_row + jnp.dot(wk8, k, preferred_element_type=F32)[0:1, :]
            new_carry.append((n_new, m_new))
        return tuple(new_carry)

    init = tuple((jnp.zeros((1, dk), F32), jnp.zeros((1, 1), F32)) for _ in range(MLSTM_HEADS))
    lax.fori_loop(0, seq // lc, chunk, init)


def _mlstm(mqk, mv, mo, sg, batch, seq):
    mqk3 = mqk.reshape(batch, seq, 2 * MLSTM_WIDTH)
    mv3 = mv.reshape(batch, seq, MLSTM_WIDTH)
    mo3 = mo.reshape(batch, seq, MLSTM_WIDTH)

    def seq_blk(w):
        return pl.BlockSpec((1, seq, w), lambda b: (b, 0, 0))

    return pl.pallas_call(
        functools.partial(_mlstm_kernel, seq=seq),
        grid=(batch,),
        in_specs=[seq_blk(2 * MLSTM_WIDTH), seq_blk(MLSTM_WIDTH), seq_blk(MLSTM_WIDTH),
                  pl.BlockSpec((SG_ROWS, seq), lambda b: (0, b))],
        out_specs=seq_blk(MLSTM_WIDTH),
        out_shape=jax.ShapeDtypeStruct((batch, seq, MLSTM_WIDTH), BF16),
        scratch_shapes=[pltpu.VMEM((MLSTM_HEADS, MLSTM_HEAD_DIM, MLSTM_HEAD_DIM), F32)],
        compiler_params=pltpu.CompilerParams(dimension_semantics=("arbitrary",),
                                             vmem_limit_bytes=VMEM_LIMIT),
        name="mlstm",
    )(mqk3, mv3, mo3, sg)


def _merge_kernel(x_ref, a_ref, hm_ref, gates_ref, wuf_ref, wum_ref, wo_ref, o_ref, mg_scr):
    for c in range(D_MODEL // COL_CHUNK):
        lo = c * COL_CHUNK
        a = jnp.dot(a_ref[...], wuf_ref[:, lo:lo + COL_CHUNK], preferred_element_type=F32)
        bm = jnp.dot(hm_ref[...], wum_ref[:, lo:lo + COL_CHUNK], preferred_element_type=F32)
        ga = gates_ref[:, lo:lo + COL_CHUNK].astype(F32)
        gb = gates_ref[:, D_MODEL + lo:D_MODEL + lo + COL_CHUNK].astype(F32)
        mg_scr[:, lo:lo + COL_CHUNK] = (_sigmoid(ga) * a + _sigmoid(gb) * bm).astype(BF16)
    for c in range(D_MODEL // COL_CHUNK):
        lo = c * COL_CHUNK
        y = jnp.dot(mg_scr[...], wo_ref[:, lo:lo + COL_CHUNK], preferred_element_type=F32)
        o_ref[:, lo:lo + COL_CHUNK] = x_ref[:, lo:lo + COL_CHUNK] + y


def _merge(x, a, hm, gates, wuf, wum, wo):
    t = x.shape[0]

    def row(w):
        return pl.BlockSpec((ROW_TILE, w), lambda i: (i, 0))

    return pl.pallas_call(
        _merge_kernel,
        grid=(t // ROW_TILE,),
        in_specs=[row(D_MODEL), row(FOX_WIDTH), row(MLSTM_WIDTH), row(2 * D_MODEL),
                  _resident(wuf.shape), _resident(wum.shape), _resident(wo.shape)],
        out_specs=row(D_MODEL),
        out_shape=jax.ShapeDtypeStruct((t, D_MODEL), F32),
        scratch_shapes=[pltpu.VMEM((ROW_TILE, D_MODEL), BF16)],
        compiler_params=pltpu.CompilerParams(dimension_semantics=("arbitrary",),
                                             vmem_limit_bytes=VMEM_LIMIT),
        name="merge",
    )(x, a, hm, gates, wuf, wum, wo)


def kernel(x, ffn1_norm, ffn1_w_in, ffn1_w_out, mix_norm, w_in, fox_f_bias, mlstm_conv_w, mlstm_conv_b,
           mlstm_i_bias, mlstm_f_bias, w_up_fox, w_up_mlstm, w_out, ffn2_norm, ffn2_w_in, ffn2_w_out,
           final_norm):
    batch, seq, d = x.shape
    depth = ffn1_norm.shape[0]
    assert depth >= 1
    assert d == D_MODEL and seq % ROW_TILE == 0 and seq % ATT_TK == 0 and seq % MLSTM_CHUNK == 0
    xf = x.reshape(batch * seq, d)
    fg = final_norm.reshape(1, d)

    o_ff = 3 * FOX_WIDTH
    o_mqk = o_ff + FOX_HEADS
    o_mv = o_mqk + 2 * MLSTM_WIDTH
    o_mi = o_mv + MLSTM_WIDTH
    o_mf = o_mi + MLSTM_HEADS
    o_mo = o_mf + MLSTM_HEADS
    o_g = o_mo + MLSTM_WIDTH

    for l in range(depth):
        w = w_in[l]
        wa = jnp.concatenate([w[:, :FOX_WIDTH] * FOX_HEAD_DIM ** -0.5, w[:, FOX_WIDTH:o_ff]], axis=1).astype(BF16)
        wm = jnp.concatenate([w[:, o_mqk:o_mi], w[:, o_mo:o_g]], axis=1).astype(BF16)
        wg = w[:, o_g:].astype(BF16)
        ws = jnp.concatenate([w[:, o_ff:o_mqk], w[:, o_mi:o_mo]], axis=1).T.astype(BF16)
        bs = jnp.concatenate([fox_f_bias[l], mlstm_i_bias[l], mlstm_f_bias[l]]).reshape(N_SMALL, 1).astype(F32)

        xf = _ffn(xf, ffn1_norm[l].reshape(1, d), ffn1_w_in[l].astype(BF16), ffn1_w_out[l].astype(BF16), fg, False)
        qkv, mqk, mv, mo, gates, sg = _inproj(
            xf, mix_norm[l].reshape(1, d), wa, wm, wg, ws, bs, mlstm_conv_w[l].astype(F32),
            mlstm_conv_b[l].reshape(1, -1).astype(F32), seq)
        att = _fox_attention(qkv, sg, batch, seq).reshape(batch * seq, FOX_WIDTH)
        hm = _mlstm(mqk, mv, mo, sg, batch, seq).reshape(batch * seq, MLSTM_WIDTH)
        xf = _merge(xf, att, hm, gates, w_up_fox[l].astype(BF16), w_up_mlstm[l].astype(BF16),
                    w_out[l].astype(BF16))
        xf = _ffn(xf, ffn2_norm[l].reshape(1, d), ffn2_w_in[l].astype(BF16), ffn2_w_out[l].astype(BF16), fg,
                  final_norm=(l == depth - 1))
    return xf.reshape(batch, seq, d)
```

```python
import functools

import jax
import jax.numpy as jnp
from jax import lax
from jax.experimental import pallas as pl
from jax.experimental.pallas import tpu as pltpu

D_MODEL = 1024
FOX_HEADS = 8
FOX_HEAD_DIM = 64
FOX_WIDTH = FOX_HEADS * FOX_HEAD_DIM
MLSTM_HEADS = 4
MLSTM_HEAD_DIM = 128
MLSTM_WIDTH = MLSTM_HEADS * MLSTM_HEAD_DIM
CONV_WIDTH = 4
D_FF = 2816
RMS_EPS = 1e-6
N_SMALL = 16
SG_ROWS = 24
SG_I = FOX_HEADS
SG_LOGF = FOX_HEADS + MLSTM_HEADS
SG_B = N_SMALL + MLSTM_HEADS

V7X_VMEM_BYTES = 64 * 1024 * 1024
LANES = 128
SUBLANES = 8

ROW_TILE = 512
FF_CHUNK = 256
COL_CHUNK = 512
ATT_TQ = 512
ATT_TK = 256
MLSTM_CHUNK = 256
VMEM_LIMIT = 56 * 1024 * 1024

F32 = jnp.float32
BF16 = jnp.bfloat16


def _rms(x, g):
    ms = jnp.mean(x * x, axis=-1, keepdims=True)
    return x * lax.rsqrt(ms + RMS_EPS) * g


def _sigmoid(x):
    return 1.0 / (1.0 + jnp.exp(-x))


def _log_sigmoid(x):
    return jnp.minimum(x, 0.0) - jnp.log1p(jnp.exp(-jnp.abs(x)))


def _resident(shape):
    nd = len(shape)
    return pl.BlockSpec(shape, lambda *_: (0,) * nd, pipeline_mode=pl.Buffered(1))


def _cumsum_lanes(x, segment=None):
    n = x.shape[-1]
    segment = n if segment is None else segment
    pos = lax.broadcasted_iota(jnp.int32, x.shape, x.ndim - 1) & (segment - 1)
    k = 1
    while k < segment:
        x = x + jnp.where(pos >= k, pltpu.roll(x, k, axis=x.ndim - 1), 0.0)
        k *= 2
    return x


def _ffn_kernel(x_ref, g_ref, win_ref, wout_ref, fg_ref, o_ref, h_scr, a_scr, *, final_norm):
    x = x_ref[...]
    h_scr[...] = _rms(x, g_ref[...]).astype(BF16)
    for f in range(D_FF // FF_CHUNK):
        lo = f * FF_CHUNK
        h = h_scr[...]
        gate = jnp.dot(h, win_ref[:, lo:lo + FF_CHUNK], preferred_element_type=F32)
        up = jnp.dot(h, win_ref[:, D_FF + lo:D_FF + lo + FF_CHUNK], preferred_element_type=F32)
        a_scr[:, lo:lo + FF_CHUNK] = (gate * _sigmoid(gate) * up).astype(BF16)
    for c in range(D_MODEL // COL_CHUNK):
        lo = c * COL_CHUNK
        y = jnp.dot(a_scr[...], wout_ref[:, lo:lo + COL_CHUNK], preferred_element_type=F32)
        o_ref[:, lo:lo + COL_CHUNK] = x_ref[:, lo:lo + COL_CHUNK] + 0.5 * y
    if final_norm:
        o_ref[...] = _rms(o_ref[...], fg_ref[...])


def _ffn(x, g, w_in, w_out, fg, final_norm):
    t = x.shape[0]
    row = pl.BlockSpec((ROW_TILE, D_MODEL), lambda i: (i, 0))
    return pl.pallas_call(
        functools.partial(_ffn_kernel, final_norm=final_norm),
        grid=(t // ROW_TILE,),
        in_specs=[row, _resident((1, D_MODEL)), _resident(w_in.shape), _resident(w_out.shape),
                  _resident((1, D_MODEL))],
        out_specs=row,
        out_shape=jax.ShapeDtypeStruct((t, D_MODEL), F32),
        scratch_shapes=[pltpu.VMEM((ROW_TILE, D_MODEL), BF16), pltpu.VMEM((ROW_TILE, D_FF), BF16)],
        compiler_params=pltpu.CompilerParams(dimension_semantics=("arbitrary",),
                                             vmem_limit_bytes=VMEM_LIMIT),
        name="ffn_final" if final_norm else "ffn",
    )(x, g, w_in, w_out, fg)


def _inproj_kernel(x_ref, g_ref, wa_ref, wm_ref, wg_ref, ws_ref, bs_ref, cw_ref, cb_ref,
                   qkv_ref, mqk_ref, mv_ref, mo_ref, gates_ref, sg_ref, h_scr, xpad, ccar, *, tiles_per_seq):
    i = pl.program_id(0)
    h_scr[...] = _rms(x_ref[...], g_ref[...]).astype(BF16)

    @pl.when(i % tiles_per_seq == 0)
    def _():
        xpad[0:SUBLANES, :] = jnp.zeros((SUBLANES, 2 * MLSTM_WIDTH), F32)
        ccar[...] = jnp.zeros(ccar.shape, F32)

    @pl.when(i % tiles_per_seq != 0)
    def _():
        xpad[0:SUBLANES, :] = xpad[ROW_TILE:ROW_TILE + SUBLANES, :]

    sg = lax.dot_general(ws_ref[...], h_scr[...], (((1,), (1,)), ((), ())),
                         preferred_element_type=F32) + bs_ref[...]
    r = lax.broadcasted_iota(jnp.int32, sg.shape, 0)
    is_i_gate = (r >= FOX_HEADS) & (r < FOX_HEADS + MLSTM_HEADS)
    sg = jnp.where(is_i_gate, sg, _log_sigmoid(sg))
    c_fox = _cumsum_lanes(sg[0:FOX_HEADS]) + ccar[:, 0:1]
    ccar[...] = jnp.broadcast_to(c_fox[:, ROW_TILE - 1:ROW_TILE], ccar.shape)
    sg_ref[0:FOX_HEADS, :] = -c_fox
    sg_ref[FOX_HEADS:N_SMALL, :] = sg[FOX_HEADS:N_SMALL]
    sg_ref[N_SMALL:SG_ROWS, :] = _cumsum_lanes(sg[FOX_HEADS:N_SMALL], segment=MLSTM_CHUNK)

    def proj(w_ref, lo):
        return jnp.dot(h_scr[...], w_ref[:, lo:lo + COL_CHUNK], preferred_element_type=F32)

    for c in range(2 * MLSTM_WIDTH // COL_CHUNK):
        xpad[SUBLANES:SUBLANES + ROW_TILE, c * COL_CHUNK:(c + 1) * COL_CHUNK] = proj(wm_ref, c * COL_CHUNK)
    qscale = MLSTM_HEAD_DIM ** -0.5

    def conv_group(lo):
        w = [cw_ref[j:j + 1, lo:lo + LANES] for j in range(CONV_WIDTH)]
        xv = xpad[:, lo:lo + LANES]
        x1 = pltpu.roll(xv, 1, axis=0)
        u = w[3] * xv + w[2] * x1
        v = w[1] * xv + w[0] * x1
        y = (u + pltpu.roll(v, 2, axis=0))[SUBLANES:, :] + cb_ref[:, lo:lo + LANES]
        y = y * _sigmoid(y)
        if lo < MLSTM_WIDTH:
            y = y * qscale
        mqk_ref[:, lo:lo + LANES] = y.astype(BF16)

    others = [(mv_ref, wm_ref, 2 * MLSTM_WIDTH, 0), (mo_ref, wm_ref, 3 * MLSTM_WIDTH, 0)]
    others += [(qkv_ref, wa_ref, c * COL_CHUNK, c * COL_CHUNK) for c in range(3 * FOX_WIDTH // COL_CHUNK)]
    others += [(gates_ref, wg_ref, c * COL_CHUNK, c * COL_CHUNK) for c in range(2 * D_MODEL // COL_CHUNK)]
    groups = list(range(0, 2 * MLSTM_WIDTH, LANES))
    for idx, (o_ref, w_ref, w_lo, o_lo) in enumerate(others):
        if idx < len(groups):
            conv_group(groups[idx])
        o_ref[:, o_lo:o_lo + COL_CHUNK] = proj(w_ref, w_lo).astype(BF16)
    for lo in groups[len(others):]:
        conv_group(lo)


def _inproj(x, g, wa, wm, wg, ws, bs, cw, cb, seq):
    t = x.shape[0]
    nt = t // ROW_TILE

    def row(w):
        return pl.BlockSpec((ROW_TILE, w), lambda i: (i, 0))

    out_shape = (
        jax.ShapeDtypeStruct((t, 3 * FOX_WIDTH), BF16),
        jax.ShapeDtypeStruct((t, 2 * MLSTM_WIDTH), BF16),
        jax.ShapeDtypeStruct((t, MLSTM_WIDTH), BF16),
        jax.ShapeDtypeStruct((t, MLSTM_WIDTH), BF16),
        jax.ShapeDtypeStruct((t, 2 * D_MODEL), BF16),
        jax.ShapeDtypeStruct((SG_ROWS, t), F32),
    )
    return pl.pallas_call(
        functools.partial(_inproj_kernel, tiles_per_seq=seq // ROW_TILE),
        grid=(nt,),
        in_specs=[row(D_MODEL), _resident((1, D_MODEL)), _resident(wa.shape), _resident(wm.shape),
                  _resident(wg.shape), _resident(ws.shape), _resident(bs.shape), _resident(cw.shape),
                  _resident(cb.shape)],
        out_specs=(row(3 * FOX_WIDTH), row(2 * MLSTM_WIDTH), row(MLSTM_WIDTH), row(MLSTM_WIDTH),
                   row(2 * D_MODEL), pl.BlockSpec((SG_ROWS, ROW_TILE), lambda i: (0, i))),
        out_shape=out_shape,
        scratch_shapes=[pltpu.VMEM((ROW_TILE, D_MODEL), BF16),
                        pltpu.VMEM((ROW_TILE + SUBLANES, 2 * MLSTM_WIDTH), F32),
                        pltpu.VMEM((FOX_HEADS, LANES), F32)],
        compiler_params=pltpu.CompilerParams(dimension_semantics=("arbitrary",),
                                             vmem_limit_bytes=VMEM_LIMIT),
        name="inproj",
    )(x, g, wa, wm, wg, ws, bs, cw, cb)


BIAS_PIECES = 3
VT_ROWS = FOX_HEAD_DIM + 16
MAX_SLAB = 64
ATT_STAGE = 256
PV_CHUNK = 512


def _fox_kernel(q_ref, k_ref, v_ref, sg_ref, o_ref, qa_scr, ka_scr, vt_scr, s_scr, p_scr, *, seq):
    hp = pl.program_id(1)
    tq = ATT_TQ
    tk = ATT_STAGE
    half = FOX_HEAD_DIM

    row = lax.broadcasted_iota(jnp.int32, (SUBLANES, seq), 0)
    blocks = []
    for hh in range(2):
        rem = sg_ref[pl.ds(2 * hp + hh, 1), :]
        blk = jnp.zeros((SUBLANES, seq), F32)
        for p in range(BIAS_PIECES):
            piece = rem.astype(BF16).astype(F32)
            blk = jnp.where(row == p, piece, blk)
            rem = rem - piece
        blocks.append(blk)
    pad = jnp.zeros((half - SUBLANES, seq), F32)
    pieces = jnp.concatenate([blocks[1], pad, blocks[0], pad], axis=0)
    lane = lax.broadcasted_iota(jnp.int32, (tk, LANES), 1)

    def lane_set(lo, hi):
        return jnp.where((lane >= lo) & (lane < hi), 1.0, 0.0).astype(BF16)

    own = (lane_set(0, half), lane_set(half, 2 * half))
    bias_lanes = (lane_set(half, half + BIAS_PIECES), lane_set(0, BIAS_PIECES))
    for c in range(seq // tk):
        sl = slice(c * tk, (c + 1) * tk)
        bias_t = pieces[:, sl].T.astype(BF16)
        k = k_ref[0, sl, :]
        q = q_ref[0, sl, :]
        for hh in range(2):
            ka_scr[hh, sl, :] = k * (1.0 - bias_lanes[hh]) + bias_t * bias_lanes[hh]
            qa_scr[hh, sl, :] = q * own[hh] + bias_lanes[hh]
        vt = v_ref[0, sl, :].astype(F32).T
        vt_scr[0, 0:half, sl] = vt[0:half].astype(BF16)
        vt_scr[1, 0:half, sl] = vt[half:2 * half].astype(BF16)
    ones_rows = jnp.where(lax.broadcasted_iota(jnp.int32, (VT_ROWS - half, seq), 0) == 0, 1.0, 0.0).astype(BF16)
    vt_scr[0, half:VT_ROWS, :] = ones_rows
    vt_scr[1, half:VT_ROWS, :] = ones_rows

    nt = (((1,), (1,)), ((), ()))
    hq = tq // 2
    tri = lax.broadcasted_iota(jnp.int32, (hq, hq), 0) <= lax.broadcasted_iota(jnp.int32, (hq, hq), 1)
    for i in range(seq // tq):
        lo, mid, n = i * tq, i * tq + hq, (i + 1) * tq
        outs = []
        for hh in range(2):
            qa = qa_scr[hh, lo:n, :]
            if lo > 0:
                s_scr[hh, 0:lo, :] = lax.dot_general(ka_scr[hh, 0:lo, :], qa, nt, preferred_element_type=F32)
            d1 = lax.dot_general(ka_scr[hh, lo:mid, :], qa, nt, preferred_element_type=F32)
            s_scr[hh, lo:mid, 0:hq] = jnp.where(tri, d1[:, 0:hq], -jnp.inf)
            s_scr[hh, lo:mid, hq:tq] = d1[:, hq:tq]
            d2 = lax.dot_general(ka_scr[hh, mid:n, :], qa[hq:tq], nt, preferred_element_type=F32)
            d2 = jnp.where(tri, d2, -jnp.inf)
            s_scr[hh, mid:n, hq:tq] = d2
            slab = jnp.max(s_scr[hh, 0:mid, :].reshape(mid // MAX_SLAB, MAX_SLAB, tq), axis=0)
            m = jnp.max(slab, axis=0, keepdims=True)
            m_right = jnp.maximum(m[:, hq:tq], jnp.max(d2, axis=0, keepdims=True))
            m = jnp.concatenate([m[:, 0:hq], m_right], axis=1)
            acc = None
            for c0 in range(0, mid, PV_CHUNK):
                c1 = min(c0 + PV_CHUNK, mid)
                p_scr[hh, c0:c1, :] = jnp.exp(s_scr[hh, c0:c1, :] - m).astype(BF16)
                part = jnp.dot(vt_scr[hh, :, c0:c1], p_scr[hh, c0:c1, :], preferred_element_type=F32)
                acc = part if acc is None else acc + part
            p_scr[hh, mid:n, hq:tq] = jnp.exp(s_scr[hh, mid:n, hq:tq] - m_right).astype(BF16)
            right = jnp.dot(vt_scr[hh, :, mid:n], p_scr[hh, mid:n, hq:tq], preferred_element_type=F32)
            acc = jnp.concatenate([acc[:, 0:hq], acc[:, hq:tq] + right], axis=1)
            outs.append(acc[0:half] * (1.0 / acc[half:half + 1]))
        o_ref[0, lo:n, :] = jnp.concatenate(outs, axis=0).T.astype(BF16)


def _fox_attention(qkv, sg, batch, seq):
    qkv3 = qkv.reshape(batch, seq, 3 * FOX_WIDTH)
    pairs = FOX_HEADS // 2

    def blk(off):
        return pl.BlockSpec((1, seq, LANES), lambda b, p: (b, 0, off + p))

    return pl.pallas_call(
        functools.partial(_fox_kernel, seq=seq),
        grid=(batch, pairs),
        in_specs=[blk(0), blk(pairs), blk(2 * pairs), pl.BlockSpec((SG_ROWS, seq), lambda b, p: (0, b))],
        out_specs=blk(0),
        out_shape=jax.ShapeDtypeStruct((batch, seq, FOX_WIDTH), BF16),
        scratch_shapes=[pltpu.VMEM((2, seq, LANES), BF16),
                        pltpu.VMEM((2, seq, LANES), BF16),
                        pltpu.VMEM((2, VT_ROWS, seq), BF16),
                        pltpu.VMEM((2, seq, ATT_TQ), F32),
                        pltpu.VMEM((2, seq, ATT_TQ), BF16)],
        compiler_params=pltpu.CompilerParams(dimension_semantics=("arbitrary", "arbitrary"),
                                             vmem_limit_bytes=VMEM_LIMIT),
        name="fox_attn",
    )(qkv3, qkv3, qkv3, sg)


def _mlstm_kernel(qk_ref, v_ref, mo_ref, sg_ref, o_ref, c_scr, *, seq):
    lc = MLSTM_CHUNK
    dk = MLSTM_HEAD_DIM
    t_idx = lax.broadcasted_iota(jnp.int32, (lc, lc), 0)
    s_idx = lax.broadcasted_iota(jnp.int32, (lc, lc), 1)
    causal = s_idx <= t_idx
    c_scr[...] = jnp.zeros(c_scr.shape, F32)

    def chunk(c, carry):
        r0 = pl.multiple_of(c * lc, lc)
        rows = pl.ds(r0, lc)
        new_carry = []
        for hd in range(MLSTM_HEADS):
            n_row, m_prev = carry[hd]
            cmat = c_scr[hd]
            q = qk_ref[0, rows, hd * dk:(hd + 1) * dk]
            k = qk_ref[0, rows, MLSTM_WIDTH + hd * dk:MLSTM_WIDTH + (hd + 1) * dk]
            v = v_ref[0, rows, hd * dk:(hd + 1) * dk]
            i_row = sg_ref[SG_I + hd:SG_I + hd + 1, rows]
            lf_row = sg_ref[SG_LOGF + hd:SG_LOGF + hd + 1, rows]
            b_row = sg_ref[SG_B + hd:SG_B + hd + 1, rows]
            b_col = jnp.sum(jnp.where(causal, lf_row, 0.0), axis=-1, keepdims=True)
            g = jnp.sum(lf_row, axis=-1, keepdims=True)
            a_row = i_row - b_row
            dmat = jnp.where(causal, b_col + a_row, -jnp.inf)
            m_inter = b_col + m_prev
            m_t = jnp.maximum(jnp.max(dmat, axis=-1, keepdims=True), m_inter)
            kt = k.astype(F32).T
            s = jnp.dot(q, kt.astype(BF16), preferred_element_type=F32) * jnp.exp(dmat - m_t)
            inter = jnp.exp(m_inter - m_t)
            num = (jnp.dot(s.astype(BF16), v, preferred_element_type=F32)
                   + inter * jnp.dot(q, cmat.astype(BF16), preferred_element_type=F32))
            qn = jnp.sum(q.astype(F32) * n_row, axis=-1, keepdims=True)
            den = jnp.sum(s, axis=-1, keepdims=True) + inter * qn
            h = num * (1.0 / jnp.maximum(jnp.abs(den), jnp.exp(-m_t)))
            gate = _sigmoid(mo_ref[0, rows, hd * dk:(hd + 1) * dk].astype(F32))
            o_ref[0, rows, hd * dk:(hd + 1) * dk] = (gate * h).astype(BF16)

            kdec = g + a_row
            m_new = jnp.maximum(g + m_prev, jnp.max(kdec, axis=-1, keepdims=True))
            wk = jnp.exp(kdec - m_new)
            cdec = jnp.exp(g + m_prev - m_new)
            kwt = (kt * wk).astype(BF16)
            c_scr[hd] = cdec * cmat + jnp.dot(kwt, v, preferred_element_type=F32)
            wk8 = jnp.broadcast_to(wk, (SUBLANES, lc)).astype(BF16)
            n_new = cdec * n_row + jnp.dot(wk8, k, preferred_element_type=F32)[0:1, :]
            new_carry.append((n_new, m_new))
        return tuple(new_carry)

    init = tuple((jnp.zeros((1, dk), F32), jnp.zeros((1, 1), F32)) for _ in range(MLSTM_HEADS))
    lax.fori_loop(0, seq // lc, chunk, init)


def _mlstm(mqk, mv, mo, sg, batch, seq):
    mqk3 = mqk.reshape(batch, seq, 2 * MLSTM_WIDTH)
    mv3 = mv.reshape(batch, seq, MLSTM_WIDTH)
    mo3 = mo.reshape(batch, seq, MLSTM_WIDTH)

    def seq_blk(w):
        return pl.BlockSpec((1, seq, w), lambda b: (b, 0, 0))

    return pl.pallas_call(
        functools.partial(_mlstm_kernel, seq=seq),
        grid=(batch,),
        in_specs=[seq_blk(2 * MLSTM_WIDTH), seq_blk(MLSTM_WIDTH), seq_blk(MLSTM_WIDTH),
                  pl.BlockSpec((SG_ROWS, seq), lambda b: (0, b))],
        out_specs=seq_blk(MLSTM_WIDTH),
        out_shape=jax.ShapeDtypeStruct((batch, seq, MLSTM_WIDTH), BF16),
        scratch_shapes=[pltpu.VMEM((MLSTM_HEADS, MLSTM_HEAD_DIM, MLSTM_HEAD_DIM), F32)],
        compiler_params=pltpu.CompilerParams(dimension_semantics=("arbitrary",),
                                             vmem_limit_bytes=VMEM_LIMIT),
        name="mlstm",
    )(mqk3, mv3, mo3, sg)


def _merge_kernel(x_ref, a_ref, hm_ref, gates_ref, wuf_ref, wum_ref, wo_ref, o_ref, mg_scr):
    for c in range(D_MODEL // COL_CHUNK):
        lo = c * COL_CHUNK
        a = jnp.dot(a_ref[...], wuf_ref[:, lo:lo + COL_CHUNK], preferred_element_type=F32)
        bm = jnp.dot(hm_ref[...], wum_ref[:, lo:lo + COL_CHUNK], preferred_element_type=F32)
        ga = gates_ref[:, lo:lo + COL_CHUNK].astype(F32)
        gb = gates_ref[:, D_MODEL + lo:D_MODEL + lo + COL_CHUNK].astype(F32)
        mg_scr[:, lo:lo + COL_CHUNK] = (_sigmoid(ga) * a + _sigmoid(gb) * bm).astype(BF16)
    for c in range(D_MODEL // COL_CHUNK):
        lo = c * COL_CHUNK
        y = jnp.dot(mg_scr[...], wo_ref[:, lo:lo + COL_CHUNK], preferred_element_type=F32)
        o_ref[:, lo:lo + COL_CHUNK] = x_ref[:, lo:lo + COL_CHUNK] + y


def _merge(x, a, hm, gates, wuf, wum, wo):
    t = x.shape[0]

    def row(w):
        return pl.BlockSpec((ROW_TILE, w), lambda i: (i, 0))

    return pl.pallas_call(
        _merge_kernel,
        grid=(t // ROW_TILE,),
        in_specs=[row(D_MODEL), row(FOX_WIDTH), row(MLSTM_WIDTH), row(2 * D_MODEL),
                  _resident(wuf.shape), _resident(wum.shape), _resident(wo.shape)],
        out_specs=row(D_MODEL),
        out_shape=jax.ShapeDtypeStruct((t, D_MODEL), F32),
        scratch_shapes=[pltpu.VMEM((ROW_TILE, D_MODEL), BF16)],
        compiler_params=pltpu.CompilerParams(dimension_semantics=("arbitrary",),
                                             vmem_limit_bytes=VMEM_LIMIT),
        name="merge",
    )(x, a, hm, gates, wuf, wum, wo)


def kernel(x, ffn1_norm, ffn1_w_in, ffn1_w_out, mix_norm, w_in, fox_f_bias, mlstm_conv_w, mlstm_conv_b,
           mlstm_i_bias, mlstm_f_bias, w_up_fox, w_up_mlstm, w_out, ffn2_norm, ffn2_w_in, ffn2_w_out,
           final_norm):
    batch, seq, d = x.shape
    depth = ffn1_norm.shape[0]
    assert depth >= 1
    assert d == D_MODEL and seq % ROW_TILE == 0 and seq % ATT_TK == 0 and seq % MLSTM_CHUNK == 0
    xf = x.reshape(batch * seq, d)
    fg = final_norm.reshape(1, d)

    o_ff = 3 * FOX_WIDTH
    o_mqk = o_ff + FOX_HEADS
    o_mv = o_mqk + 2 * MLSTM_WIDTH
    o_mi = o_mv + MLSTM_WIDTH
    o_mf = o_mi + MLSTM_HEADS
    o_mo = o_mf + MLSTM_HEADS
    o_g = o_mo + MLSTM_WIDTH

    for l in range(depth):
        w = w_in[l]
        wa = jnp.concatenate([w[:, :FOX_WIDTH] * FOX_HEAD_DIM ** -0.5, w[:, FOX_WIDTH:o_ff]], axis=1).astype(BF16)
        wm = jnp.concatenate([w[:, o_mqk:o_mi], w[:, o_mo:o_g]], axis=1).astype(BF16)
        wg = w[:, o_g:].astype(BF16)
        ws = jnp.concatenate([w[:, o_ff:o_mqk], w[:, o_mi:o_mo]], axis=1).T.astype(BF16)
        bs = jnp.concatenate([fox_f_bias[l], mlstm_i_bias[l], mlstm_f_bias[l]]).reshape(N_SMALL, 1).astype(F32)

        xf = _ffn(xf, ffn1_norm[l].reshape(1, d), ffn1_w_in[l].astype(BF16), ffn1_w_out[l].astype(BF16), fg, False)
        qkv, mqk, mv, mo, gates, sg = _inproj(
            xf, mix_norm[l].reshape(1, d), wa, wm, wg, ws, bs, mlstm_conv_w[l].astype(F32),
            mlstm_conv_b[l].reshape(1, -1).astype(F32), seq)
        att = _fox_attention(qkv, sg, batch, seq).reshape(batch * seq, FOX_WIDTH)
        hm = _mlstm(mqk, mv, mo, sg, batch, seq).reshape(batch * seq, MLSTM_WIDTH)
        xf = _merge(xf, att, hm, gates, w_up_fox[l].astype(BF16), w_up_mlstm[l].astype(BF16),
                    w_out[l].astype(BF16))
        xf = _ffn(xf, ffn2_norm[l].reshape(1, d), ffn2_w_in[l].astype(BF16), ffn2_w_out[l].astype(BF16), fg,
                  final_norm=(l == depth - 1))
    return xf.reshape(batch, seq, d)
```

```python
import functools

import jax
import jax.numpy as jnp
from jax import lax
from jax.experimental import pallas as pl
from jax.experimental.pallas import tpu as pltpu

D_MODEL = 1024
FOX_HEADS = 8
FOX_HEAD_DIM = 64
FOX_WIDTH = FOX_HEADS * FOX_HEAD_DIM
MLSTM_HEADS = 4
MLSTM_HEAD_DIM = 128
MLSTM_WIDTH = MLSTM_HEADS * MLSTM_HEAD_DIM
CONV_WIDTH = 4
D_FF = 2816
RMS_EPS = 1e-6
N_SMALL = 16
SG_ROWS = 24
SG_I = FOX_HEADS
SG_LOGF = FOX_HEADS + MLSTM_HEADS
SG_B = N_SMALL + MLSTM_HEADS

V7X_VMEM_BYTES = 64 * 1024 * 1024
LANES = 128
SUBLANES = 8

ROW_TILE = 512
FF_CHUNK = 256
COL_CHUNK = 512
ATT_TQ = 512
MLSTM_CHUNK = 256
VMEM_LIMIT = V7X_VMEM_BYTES * 7 // 8

F32 = jnp.float32
BF16 = jnp.bfloat16


def _rms(x, g):
    ms = jnp.mean(x * x, axis=-1, keepdims=True)
    return x * lax.rsqrt(ms + RMS_EPS) * g


def _sigmoid(x):
    return 1.0 / (1.0 + jnp.exp(-x))


def _log_sigmoid(x):
    return jnp.minimum(x, 0.0) - jnp.log1p(jnp.exp(-jnp.abs(x)))


def _resident(shape):
    nd = len(shape)
    return pl.BlockSpec(shape, lambda *_: (0,) * nd, pipeline_mode=pl.Buffered(1))


def _cumsum_lanes(x, segment=None):
    n = x.shape[-1]
    segment = n if segment is None else segment
    pos = lax.broadcasted_iota(jnp.int32, x.shape, x.ndim - 1) & (segment - 1)
    k = 1
    while k < segment:
        x = x + jnp.where(pos >= k, pltpu.roll(x, k, axis=x.ndim - 1), 0.0)
        k *= 2
    return x


def _swiglu_into(h_scr, a_scr, win_ref):
    for f in range(D_FF // FF_CHUNK):
        lo = f * FF_CHUNK
        h = h_scr[...]
        gate = jnp.dot(h, win_ref[:, lo:lo + FF_CHUNK], preferred_element_type=F32)
        up = jnp.dot(h, win_ref[:, D_FF + lo:D_FF + lo + FF_CHUNK], preferred_element_type=F32)
        a_scr[:, lo:lo + FF_CHUNK] = (gate * _sigmoid(gate) * up).astype(BF16)


def _ffn_kernel(x_ref, g_ref, win_ref, wout_ref, o_ref, h_scr, a_scr):
    h_scr[...] = _rms(x_ref[...], g_ref[...]).astype(BF16)
    _swiglu_into(h_scr, a_scr, win_ref)
    for c in range(D_MODEL // COL_CHUNK):
        lo = c * COL_CHUNK
        y = jnp.dot(a_scr[...], wout_ref[:, lo:lo + COL_CHUNK], preferred_element_type=F32)
        o_ref[:, lo:lo + COL_CHUNK] = x_ref[:, lo:lo + COL_CHUNK] + 0.5 * y


def _ffn(x, g, w_in, w_out):
    t = x.shape[0]
    row = pl.BlockSpec((ROW_TILE, D_MODEL), lambda i: (i, 0))
    return pl.pallas_call(
        _ffn_kernel,
        grid=(t // ROW_TILE,),
        in_specs=[row, _resident((1, D_MODEL)), _resident(w_in.shape), _resident(w_out.shape)],
        out_specs=row,
        out_shape=jax.ShapeDtypeStruct((t, D_MODEL), F32),
        scratch_shapes=[pltpu.VMEM((ROW_TILE, D_MODEL), BF16), pltpu.VMEM((ROW_TILE, D_FF), BF16)],
        compiler_params=pltpu.CompilerParams(dimension_semantics=("arbitrary",),
                                             vmem_limit_bytes=VMEM_LIMIT),
        name="ffn",
    )(x, g, w_in, w_out)


def _inproj_kernel(x_ref, g_ref, wa_ref, wm_ref, wg_ref, ws_ref, bs_ref, cw_ref, cb_ref,
                   qkv_ref, mqk_ref, mv_ref, mo_ref, gates_ref, sg_ref, h_scr, xpad, ccar, *,
                   tiles_per_seq, n_tiles):
    i = pl.program_id(0)
    slot = lax.rem(i, 2)
    prev = 1 - slot
    live = i < n_tiles

    @pl.when(i == 0)
    def _():
        xpad[1] = jnp.zeros(xpad.shape[1:], F32)

    @pl.when(i % tiles_per_seq == 0)
    def _():
        xpad[slot, 0:SUBLANES, :] = jnp.zeros((SUBLANES, 2 * MLSTM_WIDTH), F32)

    @pl.when((i % tiles_per_seq == 0) & live)
    def _():
        ccar[1] = jnp.zeros(ccar.shape[1:], F32)

    @pl.when(i % tiles_per_seq != 0)
    def _():
        xpad[slot, 0:SUBLANES, :] = xpad[prev, ROW_TILE:ROW_TILE + SUBLANES, :]

    h_scr[...] = _rms(x_ref[...], g_ref[...]).astype(BF16)

    sg = lax.dot_general(ws_ref[...], h_scr[...], (((1,), (1,)), ((), ())),
                         preferred_element_type=F32) + bs_ref[...]
    r = lax.broadcasted_iota(jnp.int32, sg.shape, 0)
    is_i_gate = (r >= FOX_HEADS) & (r < FOX_HEADS + MLSTM_HEADS)
    sg = jnp.where(is_i_gate, sg, _log_sigmoid(sg))
    carry_in = jnp.where(live, ccar[1], ccar[0])
    c_fox = _cumsum_lanes(sg[0:FOX_HEADS]) + carry_in[:, 0:1]
    ccar[0] = carry_in
    ccar[1] = jnp.broadcast_to(c_fox[:, ROW_TILE - 1:ROW_TILE], carry_in.shape)
    sg_ref[0:FOX_HEADS, :] = -c_fox
    sg_ref[FOX_HEADS:N_SMALL, :] = sg[FOX_HEADS:N_SMALL]
    sg_ref[N_SMALL:SG_ROWS, :] = _cumsum_lanes(sg[FOX_HEADS:N_SMALL], segment=MLSTM_CHUNK)

    def proj(w_ref, lo):
        return jnp.dot(h_scr[...], w_ref[:, lo:lo + COL_CHUNK], preferred_element_type=F32)

    qscale = MLSTM_HEAD_DIM ** -0.5

    def conv_group(lo, after):
        w = [cw_ref[j:j + 1, lo:lo + LANES] for j in range(CONV_WIDTH)]
        xv = xpad[prev, :, lo:lo + LANES]
        x1 = pltpu.roll(xv, 1, axis=0)
        u = w[3] * xv + w[2] * x1
        v = w[1] * xv + w[0] * x1
        y = (u + pltpu.roll(v, 2, axis=0))[SUBLANES:, :] + (cb_ref[:, lo:lo + LANES] + after)
        y = y * _sigmoid(y)
        if lo < MLSTM_WIDTH:
            y = y * qscale
        mqk_ref[:, lo:lo + LANES] = y.astype(BF16)

    def zero_of(res):
        return jnp.minimum(jnp.abs(res[0:SUBLANES, 0:LANES]), 0.0)[0:1, :]

    groups = list(range(0, 2 * MLSTM_WIDTH, LANES))
    for c in range(2 * MLSTM_WIDTH // COL_CHUNK):
        xpad[slot, SUBLANES:SUBLANES + ROW_TILE, c * COL_CHUNK:(c + 1) * COL_CHUNK] = proj(wm_ref, c * COL_CHUNK)
    others = [(mv_ref, wm_ref, 2 * MLSTM_WIDTH, 0), (mo_ref, wm_ref, 3 * MLSTM_WIDTH, 0)]
    others += [(qkv_ref, wa_ref, c * COL_CHUNK, c * COL_CHUNK) for c in range(3 * FOX_WIDTH // COL_CHUNK)]
    others += [(gates_ref, wg_ref, c * COL_CHUNK, c * COL_CHUNK) for c in range(2 * D_MODEL // COL_CHUNK)]
    assert len(others) >= len(groups)
    for idx, (o_ref, w_ref, w_lo, o_lo) in enumerate(others):
        res = proj(w_ref, w_lo)
        o_ref[:, o_lo:o_lo + COL_CHUNK] = res.astype(BF16)
        if idx < len(groups):
            conv_group(groups[idx], zero_of(res))


def _inproj(x, g, wa, wm, wg, ws, bs, cw, cb, seq):
    t = x.shape[0]
    nt = t // ROW_TILE

    def row(w):
        return pl.BlockSpec((ROW_TILE, w), lambda i: (jnp.minimum(i, nt - 1), 0))

    out_shape = (
        jax.ShapeDtypeStruct((t, 3 * FOX_WIDTH), BF16),
        jax.ShapeDtypeStruct((t, 2 * MLSTM_WIDTH), BF16),
        jax.ShapeDtypeStruct((t, MLSTM_WIDTH), BF16),
        jax.ShapeDtypeStruct((t, MLSTM_WIDTH), BF16),
        jax.ShapeDtypeStruct((t, 2 * D_MODEL), BF16),
        jax.ShapeDtypeStruct((SG_ROWS, t), F32),
    )
    return pl.pallas_call(
        functools.partial(_inproj_kernel, tiles_per_seq=seq // ROW_TILE, n_tiles=nt),
        grid=(nt + 1,),
        in_specs=[row(D_MODEL), _resident((1, D_MODEL)), _resident(wa.shape), _resident(wm.shape),
                  _resident(wg.shape), _resident(ws.shape), _resident(bs.shape), _resident(cw.shape),
                  _resident(cb.shape)],
        out_specs=(row(3 * FOX_WIDTH),
                   pl.BlockSpec((ROW_TILE, 2 * MLSTM_WIDTH), lambda i: (jnp.maximum(i - 1, 0), 0)),
                   row(MLSTM_WIDTH), row(MLSTM_WIDTH), row(2 * D_MODEL),
                   pl.BlockSpec((SG_ROWS, ROW_TILE), lambda i: (0, jnp.minimum(i, nt - 1)))),
        out_shape=out_shape,
        scratch_shapes=[pltpu.VMEM((ROW_TILE, D_MODEL), BF16),
                        pltpu.VMEM((2, ROW_TILE + SUBLANES, 2 * MLSTM_WIDTH), F32),
                        pltpu.VMEM((2, FOX_HEADS, LANES), F32)],
        compiler_params=pltpu.CompilerParams(dimension_semantics=("arbitrary",),
                                             vmem_limit_bytes=VMEM_LIMIT),
        name="inproj",
    )(x, g, wa, wm, wg, ws, bs, cw, cb)


BIAS_PIECES = 3
VT_ROWS = FOX_HEAD_DIM + 16
MAX_SLAB = 64
ATT_STAGE = 256
PV_CHUNK = 512


def _fox_kernel(q_ref, k_ref, v_ref, sg_ref, o_ref, qa_scr, ka_scr, vt_scr, s_scr, p_scr, *, seq):
    hp = pl.program_id(1)
    tq = ATT_TQ
    tk = ATT_STAGE
    half = FOX_HEAD_DIM

    row = lax.broadcasted_iota(jnp.int32, (SUBLANES, seq), 0)
    blocks = []
    for hh in range(2):
        rem = sg_ref[pl.ds(2 * hp + hh, 1), :]
        blk = jnp.zeros((SUBLANES, seq), F32)
        for p in range(BIAS_PIECES):
            piece = rem.astype(BF16).astype(F32)
            blk = jnp.where(row == p, piece, blk)
            rem = rem - piece
        blocks.append(blk)
    pad = jnp.zeros((half - SUBLANES, seq), F32)
    pieces = jnp.concatenate([blocks[1], pad, blocks[0], pad], axis=0)
    lane = lax.broadcasted_iota(jnp.int32, (tk, LANES), 1)

    def lane_set(lo, hi):
        return jnp.where((lane >= lo) & (lane < hi), 1.0, 0.0).astype(BF16)

    own = (lane_set(0, half), lane_set(half, 2 * half))
    bias_lanes = (lane_set(half, half + BIAS_PIECES), lane_set(0, BIAS_PIECES))
    for c in range(seq // tk):
        sl = slice(c * tk, (c + 1) * tk)
        bias_t = pieces[:, sl].T.astype(BF16)
        k = k_ref[0, sl, :]
        q = q_ref[0, sl, :]
        for hh in range(2):
            ka_scr[hh, sl, :] = k * (1.0 - bias_lanes[hh]) + bias_t * bias_lanes[hh]
            qa_scr[hh, sl, :] = q * own[hh] + bias_lanes[hh]
        vt = v_ref[0, sl, :].astype(F32).T
        vt_scr[0, 0:half, sl] = vt[0:half].astype(BF16)
        vt_scr[1, 0:half, sl] = vt[half:2 * half].astype(BF16)
    ones_rows = jnp.where(lax.broadcasted_iota(jnp.int32, (VT_ROWS - half, seq), 0) == 0, 1.0, 0.0).astype(BF16)
    vt_scr[0, half:VT_ROWS, :] = ones_rows
    vt_scr[1, half:VT_ROWS, :] = ones_rows

    nt = (((1,), (1,)), ((), ()))
    hq = tq // 2
    tri = lax.broadcasted_iota(jnp.int32, (hq, hq), 0) <= lax.broadcasted_iota(jnp.int32, (hq, hq), 1)
    for i in range(seq // tq):
        lo, mid, n = i * tq, i * tq + hq, (i + 1) * tq
        outs = []
        for hh in range(2):
            qa = qa_scr[hh, lo:n, :]
            if lo > 0:
                s_scr[hh, 0:lo, :] = lax.dot_general(ka_scr[hh, 0:lo, :], qa, nt, preferred_element_type=F32)
            d1 = lax.dot_general(ka_scr[hh, lo:mid, :], qa, nt, preferred_element_type=F32)
            s_scr[hh, lo:mid, 0:hq] = jnp.where(tri, d1[:, 0:hq], -jnp.inf)
            s_scr[hh, lo:mid, hq:tq] = d1[:, hq:tq]
            d2 = lax.dot_general(ka_scr[hh, mid:n, :], qa[hq:tq], nt, preferred_element_type=F32)
            d2 = jnp.where(tri, d2, -jnp.inf)
            s_scr[hh, mid:n, hq:tq] = d2
            slab = jnp.max(s_scr[hh, 0:mid, :].reshape(mid // MAX_SLAB, MAX_SLAB, tq), axis=0)
            m = jnp.max(slab, axis=0, keepdims=True)
            m_right = jnp.maximum(m[:, hq:tq], jnp.max(d2, axis=0, keepdims=True))
            m = jnp.concatenate([m[:, 0:hq], m_right], axis=1)
            acc = None
            for c0 in range(0, mid, PV_CHUNK):
                c1 = min(c0 + PV_CHUNK, mid)
                p_scr[hh, c0:c1, :] = jnp.exp(s_scr[hh, c0:c1, :] - m).astype(BF16)
                part = jnp.dot(vt_scr[hh, :, c0:c1], p_scr[hh, c0:c1, :], preferred_element_type=F32)
                acc = part if acc is None else acc + part
            p_scr[hh, mid:n, hq:tq] = jnp.exp(s_scr[hh, mid:n, hq:tq] - m_right).astype(BF16)
            right = jnp.dot(vt_scr[hh, :, mid:n], p_scr[hh, mid:n, hq:tq], preferred_element_type=F32)
            acc = jnp.concatenate([acc[:, 0:hq], acc[:, hq:tq] + right], axis=1)
            outs.append(acc[0:half] * (1.0 / acc[half:half + 1]))
        o_ref[0, lo:n, :] = jnp.concatenate(outs, axis=0).T.astype(BF16)


def _fox_attention(qkv, sg, batch, seq):
    qkv3 = qkv.reshape(batch, seq, 3 * FOX_WIDTH)
    pairs = FOX_HEADS // 2

    def blk(off):
        return pl.BlockSpec((1, seq, LANES), lambda b, p: (b, 0, off + p))

    return pl.pallas_call(
        functools.partial(_fox_kernel, seq=seq),
        grid=(batch, pairs),
        in_specs=[blk(0), blk(pairs), blk(2 * pairs), pl.BlockSpec((SG_ROWS, seq), lambda b, p: (0, b))],
        out_specs=blk(0),
        out_shape=jax.ShapeDtypeStruct((batch, seq, FOX_WIDTH), BF16),
        scratch_shapes=[pltpu.VMEM((2, seq, LANES), BF16),
                        pltpu.VMEM((2, seq, LANES), BF16),
                        pltpu.VMEM((2, VT_ROWS, seq), BF16),
                        pltpu.VMEM((2, seq, ATT_TQ), F32),
                        pltpu.VMEM((2, seq, ATT_TQ), BF16)],
        compiler_params=pltpu.CompilerParams(dimension_semantics=("arbitrary", "arbitrary"),
                                             vmem_limit_bytes=VMEM_LIMIT),
        name="fox_attn",
    )(qkv3, qkv3, qkv3, sg)


def _mlstm_kernel(qk_ref, v_ref, mo_ref, sg_ref, o_ref, c_scr, *, seq):
    lc = MLSTM_CHUNK
    dk = MLSTM_HEAD_DIM
    t_idx = lax.broadcasted_iota(jnp.int32, (lc, lc), 0)
    s_idx = lax.broadcasted_iota(jnp.int32, (lc, lc), 1)
    causal = s_idx <= t_idx
    c_scr[...] = jnp.zeros(c_scr.shape, F32)

    def chunk(c, carry):
        r0 = pl.multiple_of(c * lc, lc)
        rows = pl.ds(r0, lc)
        new_carry = []
        for hd in range(MLSTM_HEADS):
            n_row, m_prev = carry[hd]
            cmat = c_scr[hd]
            q = qk_ref[0, rows, hd * dk:(hd + 1) * dk]
            k = qk_ref[0, rows, MLSTM_WIDTH + hd * dk:MLSTM_WIDTH + (hd + 1) * dk]
            v = v_ref[0, rows, hd * dk:(hd + 1) * dk]
            i_row = sg_ref[SG_I + hd:SG_I + hd + 1, rows]
            lf_row = sg_ref[SG_LOGF + hd:SG_LOGF + hd + 1, rows]
            b_row = sg_ref[SG_B + hd:SG_B + hd + 1, rows]
            b_col = jnp.sum(jnp.where(causal, lf_row, 0.0), axis=-1, keepdims=True)
            g = jnp.sum(lf_row, axis=-1, keepdims=True)
            a_row = i_row - b_row
            dmat = jnp.where(causal, b_col + a_row, -jnp.inf)
            m_inter = b_col + m_prev
            m_t = jnp.maximum(jnp.max(dmat, axis=-1, keepdims=True), m_inter)
            kt = k.astype(F32).T
            s = jnp.dot(q, kt.astype(BF16), preferred_element_type=F32) * jnp.exp(dmat - m_t)
            inter = jnp.exp(m_inter - m_t)
            num = (jnp.dot(s.astype(BF16), v, preferred_element_type=F32)
                   + inter * jnp.dot(q, cmat.astype(BF16), preferred_element_type=F32))
            qn = jnp.sum(q.astype(F32) * n_row, axis=-1, keepdims=True)
            den = jnp.sum(s, axis=-1, keepdims=True) + inter * qn
            h = num * (1.0 / jnp.maximum(jnp.abs(den), jnp.exp(-m_t)))
            gate = _sigmoid(mo_ref[0, rows, hd * dk:(hd + 1) * dk].astype(F32))
            o_ref[0, rows, hd * dk:(hd + 1) * dk] = (gate * h).astype(BF16)

            kdec = g + a_row
            m_new = jnp.maximum(g + m_prev, jnp.max(kdec, axis=-1, keepdims=True))
            wk = jnp.exp(kdec - m_new)
            cdec = jnp.exp(g + m_prev - m_new)
            kwt = (kt * wk).astype(BF16)
            c_scr[hd] = cdec * cmat + jnp.dot(kwt, v, preferred_element_type=F32)
            wk8 = jnp.broadcast_to(wk, (SUBLANES, lc)).astype(BF16)
            n_new = cdec * n_row + jnp.dot(wk8, k, preferred_element_type=F32)[0:1, :]
            new_carry.append((n_new, m_new))
        return tuple(new_carry)

    init = tuple((jnp.zeros((1, dk), F32), jnp.zeros((1, 1), F32)) for _ in range(MLSTM_HEADS))
    lax.fori_loop(0, seq // lc, chunk, init)


def _mlstm(mqk, mv, mo, sg, batch, seq):
    mqk3 = mqk.reshape(batch, seq, 2 * MLSTM_WIDTH)
    mv3 = mv.reshape(batch, seq, MLSTM_WIDTH)
    mo3 = mo.reshape(batch, seq, MLSTM_WIDTH)

    def seq_blk(w):
        return pl.BlockSpec((1, seq, w), lambda b: (b, 0, 0))

    return pl.pallas_call(
        functools.partial(_mlstm_kernel, seq=seq),
        grid=(batch,),
        in_specs=[seq_blk(2 * MLSTM_WIDTH), seq_blk(MLSTM_WIDTH), seq_blk(MLSTM_WIDTH),
                  pl.BlockSpec((SG_ROWS, seq), lambda b: (0, b))],
        out_specs=seq_blk(MLSTM_WIDTH),
        out_shape=jax.ShapeDtypeStruct((batch, seq, MLSTM_WIDTH), BF16),
        scratch_shapes=[pltpu.VMEM((MLSTM_HEADS, MLSTM_HEAD_DIM, MLSTM_HEAD_DIM), F32)],
        compiler_params=pltpu.CompilerParams(dimension_semantics=("arbitrary",),
                                             vmem_limit_bytes=VMEM_LIMIT),
        name="mlstm",
    )(mqk3, mv3, mo3, sg)


def _merge_ffn_kernel(x_ref, a_ref, hm_ref, gates_ref, wuf_ref, wum_ref, wo_ref, g_ref, win_ref, wout_ref,
                      fg_ref, o_ref, mg_scr, x2_scr, h_scr, a_scr, *, final_norm):
    for c in range(D_MODEL // COL_CHUNK):
        lo = c * COL_CHUNK
        a = jnp.dot(a_ref[...], wuf_ref[:, lo:lo + COL_CHUNK], preferred_element_type=F32)
        bm = jnp.dot(hm_ref[...], wum_ref[:, lo:lo + COL_CHUNK], preferred_element_type=F32)
        ga = gates_ref[:, lo:lo + COL_CHUNK].astype(F32)
        gb = gates_ref[:, D_MODEL + lo:D_MODEL + lo + COL_CHUNK].astype(F32)
        mg_scr[:, lo:lo + COL_CHUNK] = (_sigmoid(ga) * a + _sigmoid(gb) * bm).astype(BF16)
    for c in range(D_MODEL // COL_CHUNK):
        lo = c * COL_CHUNK
        y = jnp.dot(mg_scr[...], wo_ref[:, lo:lo + COL_CHUNK], preferred_element_type=F32)
        x2_scr[:, lo:lo + COL_CHUNK] = x_ref[:, lo:lo + COL_CHUNK] + y
    h_scr[...] = _rms(x2_scr[...], g_ref[...]).astype(BF16)
    _swiglu_into(h_scr, a_scr, win_ref)
    for c in range(D_MODEL // COL_CHUNK):
        lo = c * COL_CHUNK
        y = jnp.dot(a_scr[...], wout_ref[:, lo:lo + COL_CHUNK], preferred_element_type=F32)
        o_ref[:, lo:lo + COL_CHUNK] = x2_scr[:, lo:lo + COL_CHUNK] + 0.5 * y
    if final_norm:
        o_ref[...] = _rms(o_ref[...], fg_ref[...])


def _merge_ffn(x, a, hm, gates, wuf, wum, wo, g, w_in, w_out, fg, final_norm):
    t = x.shape[0]

    def row(w):
        return pl.BlockSpec((ROW_TILE, w), lambda i: (i, 0))

    return pl.pallas_call(
        functools.partial(_merge_ffn_kernel, final_norm=final_norm),
        grid=(t // ROW_TILE,),
        in_specs=[row(D_MODEL), row(FOX_WIDTH), row(MLSTM_WIDTH), row(2 * D_MODEL),
                  _resident(wuf.shape), _resident(wum.shape), _resident(wo.shape), _resident((1, D_MODEL)),
                  _resident(w_in.shape), _resident(w_out.shape), _resident((1, D_MODEL))],
        out_specs=row(D_MODEL),
        out_shape=jax.ShapeDtypeStruct((t, D_MODEL), F32),
        scratch_shapes=[pltpu.VMEM((ROW_TILE, D_MODEL), BF16), pltpu.VMEM((ROW_TILE, D_MODEL), F32),
                        pltpu.VMEM((ROW_TILE, D_MODEL), BF16), pltpu.VMEM((ROW_TILE, D_FF), BF16)],
        compiler_params=pltpu.CompilerParams(dimension_semantics=("arbitrary",),
                                             vmem_limit_bytes=VMEM_LIMIT),
        name="merge_ffn",
    )(x, a, hm, gates, wuf, wum, wo, g, w_in, w_out, fg)


def kernel(x, ffn1_norm, ffn1_w_in, ffn1_w_out, mix_norm, w_in, fox_f_bias, mlstm_conv_w, mlstm_conv_b,
           mlstm_i_bias, mlstm_f_bias, w_up_fox, w_up_mlstm, w_out, ffn2_norm, ffn2_w_in, ffn2_w_out,
           final_norm):
    batch, seq, d = x.shape
    depth = ffn1_norm.shape[0]
    assert depth >= 1
    assert d == D_MODEL and seq % ROW_TILE == 0 and seq % ATT_TQ == 0 and seq % MLSTM_CHUNK == 0
    assert ROW_TILE % MLSTM_CHUNK == 0 and mlstm_conv_w.shape[1] == CONV_WIDTH
    xf = x.reshape(batch * seq, d)
    fg = final_norm.reshape(1, d)

    o_ff = 3 * FOX_WIDTH
    o_mqk = o_ff + FOX_HEADS
    o_mv = o_mqk + 2 * MLSTM_WIDTH
    o_mi = o_mv + MLSTM_WIDTH
    o_mf = o_mi + MLSTM_HEADS
    o_mo = o_mf + MLSTM_HEADS
    o_g = o_mo + MLSTM_WIDTH

    for l in range(depth):
        w = w_in[l]
        wa = jnp.concatenate([w[:, :FOX_WIDTH] * FOX_HEAD_DIM ** -0.5, w[:, FOX_WIDTH:o_ff]], axis=1).astype(BF16)
        wm = jnp.concatenate([w[:, o_mqk:o_mi], w[:, o_mo:o_g]], axis=1).astype(BF16)
        wg = w[:, o_g:].astype(BF16)
        ws = jnp.concatenate([w[:, o_ff:o_mqk], w[:, o_mi:o_mo]], axis=1).T.astype(BF16)
        bs = jnp.concatenate([fox_f_bias[l], mlstm_i_bias[l], mlstm_f_bias[l]]).reshape(N_SMALL, 1).astype(F32)

        xf = _ffn(xf, ffn1_norm[l].reshape(1, d), ffn1_w_in[l].astype(BF16), ffn1_w_out[l].astype(BF16))
        qkv, mqk, mv, mo, gates, sg = _inproj(
            xf, mix_norm[l].reshape(1, d), wa, wm, wg, ws, bs, mlstm_conv_w[l].astype(F32),
            mlstm_conv_b[l].reshape(1, -1).astype(F32), seq)
        att = _fox_attention(qkv, sg, batch, seq).reshape(batch * seq, FOX_WIDTH)
        hm = _mlstm(mqk, mv, mo, sg, batch, seq).reshape(batch * seq, MLSTM_WIDTH)
        xf = _merge_ffn(xf, att, hm, gates, w_up_fox[l].astype(BF16), w_up_mlstm[l].astype(BF16),
                        w_out[l].astype(BF16), ffn2_norm[l].reshape(1, d), ffn2_w_in[l].astype(BF16),
                        ffn2_w_out[l].astype(BF16), fg, final_norm=(l == depth - 1))
    return xf.reshape(batch, seq, d)
```

```python
import functools

import jax
import jax.numpy as jnp
from jax import lax
from jax.experimental import pallas as pl
from jax.experimental.pallas import tpu as pltpu

D_MODEL = 1024
FOX_HEADS = 8
FOX_HEAD_DIM = 64
FOX_WIDTH = FOX_HEADS * FOX_HEAD_DIM
MLSTM_HEADS = 4
MLSTM_HEAD_DIM = 128
MLSTM_WIDTH = MLSTM_HEADS * MLSTM_HEAD_DIM
CONV_WIDTH = 4
D_FF = 2816
RMS_EPS = 1e-6
N_SMALL = 16
SG_ROWS = 24
SG_I = FOX_HEADS
SG_LOGF = FOX_HEADS + MLSTM_HEADS
SG_B = N_SMALL + MLSTM_HEADS

V7X_VMEM_BYTES = 64 * 1024 * 1024
LANES = 128
SUBLANES = 8

ROW_TILE = 512
FF_CHUNK = 256
COL_CHUNK = 512
ATT_TQ = 512
MLSTM_CHUNK = 256
MLSTM_SEQS = 2
VMEM_LIMIT = V7X_VMEM_BYTES * 7 // 8

F32 = jnp.float32
BF16 = jnp.bfloat16


def _rms(x, g):
    ms = jnp.mean(x * x, axis=-1, keepdims=True)
    return x * lax.rsqrt(ms + RMS_EPS) * g


def _sigmoid(x):
    return 1.0 / (1.0 + jnp.exp(-x))


def _log_sigmoid(x):
    return jnp.minimum(x, 0.0) - jnp.log1p(jnp.exp(-jnp.abs(x)))


def _resident(shape):
    nd = len(shape)
    return pl.BlockSpec(shape, lambda *_: (0,) * nd, pipeline_mode=pl.Buffered(1))


def _cumsum_lanes(x, segment=None):
    n = x.shape[-1]
    segment = n if segment is None else segment
    pos = lax.broadcasted_iota(jnp.int32, x.shape, x.ndim - 1) & (segment - 1)
    k = 1
    while k < segment:
        x = x + jnp.where(pos >= k, pltpu.roll(x, k, axis=x.ndim - 1), 0.0)
        k *= 2
    return x


def _swiglu_into(h_scr, a_scr, win_ref):
    for f in range(D_FF // FF_CHUNK):
        lo = f * FF_CHUNK
        h = h_scr[...]
        gate = jnp.dot(h, win_ref[:, lo:lo + FF_CHUNK], preferred_element_type=F32)
        up = jnp.dot(h, win_ref[:, D_FF + lo:D_FF + lo + FF_CHUNK], preferred_element_type=F32)
        a_scr[:, lo:lo + FF_CHUNK] = (gate * _sigmoid(gate) * up).astype(BF16)


def _ffn_kernel(x_ref, g_ref, win_ref, wout_ref, o_ref, h_scr, a_scr):
    h_scr[...] = _rms(x_ref[...], g_ref[...]).astype(BF16)
    _swiglu_into(h_scr, a_scr, win_ref)
    for c in range(D_MODEL // COL_CHUNK):
        lo = c * COL_CHUNK
        y = jnp.dot(a_scr[...], wout_ref[:, lo:lo + COL_CHUNK], preferred_element_type=F32)
        o_ref[:, lo:lo + COL_CHUNK] = x_ref[:, lo:lo + COL_CHUNK] + 0.5 * y


def _ffn(x, g, w_in, w_out):
    t = x.shape[0]
    row = pl.BlockSpec((ROW_TILE, D_MODEL), lambda i: (i, 0))
    return pl.pallas_call(
        _ffn_kernel,
        grid=(t // ROW_TILE,),
        in_specs=[row, _resident((1, D_MODEL)), _resident(w_in.shape), _resident(w_out.shape)],
        out_specs=row,
        out_shape=jax.ShapeDtypeStruct((t, D_MODEL), F32),
        scratch_shapes=[pltpu.VMEM((ROW_TILE, D_MODEL), BF16), pltpu.VMEM((ROW_TILE, D_FF), BF16)],
        compiler_params=pltpu.CompilerParams(dimension_semantics=("arbitrary",),
                                             vmem_limit_bytes=VMEM_LIMIT),
        name="ffn",
    )(x, g, w_in, w_out)


def _inproj_kernel(x_ref, g_ref, wa_ref, wm_ref, wg_ref, ws_ref, bs_ref, cw_ref, cb_ref,
                   qkv_ref, mqk_ref, mv_ref, mo_ref, gates_ref, sg_ref, h_scr, xpad, ccar, *,
                   tiles_per_seq, n_tiles):
    i = pl.program_id(0)
    slot = lax.rem(i, 2)
    prev = 1 - slot
    live = i < n_tiles

    @pl.when(i == 0)
    def _():
        xpad[1] = jnp.zeros(xpad.shape[1:], F32)

    @pl.when(i % tiles_per_seq == 0)
    def _():
        xpad[slot, 0:SUBLANES, :] = jnp.zeros((SUBLANES, 2 * MLSTM_WIDTH), F32)

    @pl.when((i % tiles_per_seq == 0) & live)
    def _():
        ccar[1] = jnp.zeros(ccar.shape[1:], F32)

    @pl.when(i % tiles_per_seq != 0)
    def _():
        xpad[slot, 0:SUBLANES, :] = xpad[prev, ROW_TILE:ROW_TILE + SUBLANES, :]

    h_scr[...] = _rms(x_ref[...], g_ref[...]).astype(BF16)

    sg = lax.dot_general(ws_ref[...], h_scr[...], (((1,), (1,)), ((), ())),
                         preferred_element_type=F32) + bs_ref[...]
    r = lax.broadcasted_iota(jnp.int32, sg.shape, 0)
    is_i_gate = (r >= FOX_HEADS) & (r < FOX_HEADS + MLSTM_HEADS)
    sg = jnp.where(is_i_gate, sg, _log_sigmoid(sg))
    carry_in = jnp.where(live, ccar[1], ccar[0])
    c_fox = _cumsum_lanes(sg[0:FOX_HEADS]) + carry_in[:, 0:1]
    ccar[0] = carry_in
    ccar[1] = jnp.broadcast_to(c_fox[:, ROW_TILE - 1:ROW_TILE], carry_in.shape)
    sg_ref[0:FOX_HEADS, :] = -c_fox
    sg_ref[FOX_HEADS:N_SMALL, :] = sg[FOX_HEADS:N_SMALL]
    sg_ref[N_SMALL:SG_ROWS, :] = _cumsum_lanes(sg[FOX_HEADS:N_SMALL], segment=MLSTM_CHUNK)

    def proj(w_ref, lo):
        return jnp.dot(h_scr[...], w_ref[:, lo:lo + COL_CHUNK], preferred_element_type=F32)

    qscale = MLSTM_HEAD_DIM ** -0.5

    def conv_group(lo, after):
        w = [cw_ref[j:j + 1, lo:lo + LANES] for j in range(CONV_WIDTH)]
        xv = xpad[prev, :, lo:lo + LANES]
        x1 = pltpu.roll(xv, 1, axis=0)
        u = w[3] * xv + w[2] * x1
        v = w[1] * xv + w[0] * x1
        y = (u + pltpu.roll(v, 2, axis=0))[SUBLANES:, :] + (cb_ref[:, lo:lo + LANES] + after)
        y = y * _sigmoid(y)
        if lo < MLSTM_WIDTH:
            y = y * qscale
        mqk_ref[:, lo:lo + LANES] = y.astype(BF16)

    def zero_of(res):
        return jnp.minimum(jnp.abs(res[0:SUBLANES, 0:LANES]), 0.0)[0:1, :]

    def store_cols(o_ref, lo):
        def store(res):
            o_ref[:, lo:lo + COL_CHUNK] = res.astype(BF16)
        return store

    def store_qkv(c):
        def store(res):
            for j in range(COL_CHUNK // LANES):
                qkv_ref[c * (COL_CHUNK // LANES) + j] = res[:, j * LANES:(j + 1) * LANES].astype(BF16)
        return store

    groups = list(range(0, 2 * MLSTM_WIDTH, LANES))
    for c in range(2 * MLSTM_WIDTH // COL_CHUNK):
        xpad[slot, SUBLANES:SUBLANES + ROW_TILE, c * COL_CHUNK:(c + 1) * COL_CHUNK] = proj(wm_ref, c * COL_CHUNK)
    others = [(store_cols(mv_ref, 0), wm_ref, 2 * MLSTM_WIDTH), (store_cols(mo_ref, 0), wm_ref, 3 * MLSTM_WIDTH)]
    others += [(store_qkv(c), wa_ref, c * COL_CHUNK) for c in range(3 * FOX_WIDTH // COL_CHUNK)]
    others += [(store_cols(gates_ref, c * COL_CHUNK), wg_ref, c * COL_CHUNK)
               for c in range(2 * D_MODEL // COL_CHUNK)]
    assert len(others) >= len(groups)
    for idx, (store, w_ref, w_lo) in enumerate(others):
        res = proj(w_ref, w_lo)
        store(res)
        if idx < len(groups):
            conv_group(groups[idx], zero_of(res))


def _inproj(x, g, wa, wm, wg, ws, bs, cw, cb, seq):
    t = x.shape[0]
    nt = t // ROW_TILE

    def row(w):
        return pl.BlockSpec((ROW_TILE, w), lambda i: (jnp.minimum(i, nt - 1), 0))

    qkv_slabs = 3 * FOX_WIDTH // LANES
    out_shape = (
        jax.ShapeDtypeStruct((qkv_slabs, t, LANES), BF16),
        jax.ShapeDtypeStruct((t, 2 * MLSTM_WIDTH), BF16),
        jax.ShapeDtypeStruct((t, MLSTM_WIDTH), BF16),
        jax.ShapeDtypeStruct((t, MLSTM_WIDTH), BF16),
        jax.ShapeDtypeStruct((t, 2 * D_MODEL), BF16),
        jax.ShapeDtypeStruct((SG_ROWS, t), F32),
    )
    return pl.pallas_call(
        functools.partial(_inproj_kernel, tiles_per_seq=seq // ROW_TILE, n_tiles=nt),
        grid=(nt + 1,),
        in_specs=[row(D_MODEL), _resident((1, D_MODEL)), _resident(wa.shape), _resident(wm.shape),
                  _resident(wg.shape), _resident(ws.shape), _resident(bs.shape), _resident(cw.shape),
                  _resident(cb.shape)],
        out_specs=(pl.BlockSpec((qkv_slabs, ROW_TILE, LANES), lambda i: (0, jnp.minimum(i, nt - 1), 0)),
                   pl.BlockSpec((ROW_TILE, 2 * MLSTM_WIDTH), lambda i: (jnp.maximum(i - 1, 0), 0)),
                   row(MLSTM_WIDTH), row(MLSTM_WIDTH), row(2 * D_MODEL),
                   pl.BlockSpec((SG_ROWS, ROW_TILE), lambda i: (0, jnp.minimum(i, nt - 1)))),
        out_shape=out_shape,
        scratch_shapes=[pltpu.VMEM((ROW_TILE, D_MODEL), BF16),
                        pltpu.VMEM((2, ROW_TILE + SUBLANES, 2 * MLSTM_WIDTH), F32),
                        pltpu.VMEM((2, FOX_HEADS, LANES), F32)],
        compiler_params=pltpu.CompilerParams(dimension_semantics=("arbitrary",),
                                             vmem_limit_bytes=VMEM_LIMIT),
        name="inproj",
    )(x, g, wa, wm, wg, ws, bs, cw, cb)


BIAS_PIECES = 3
VT_ROWS = FOX_HEAD_DIM + 16
MAX_SLAB = 64
ATT_STAGE = 256
PV_CHUNK = 512


def _fox_kernel(q_ref, k_ref, v_ref, sg_ref, o_ref, qa_scr, ka_scr, vt_scr, s_scr, p_scr, *, seq):
    hp = pl.program_id(1)
    tq = ATT_TQ
    tk = ATT_STAGE
    half = FOX_HEAD_DIM

    row = lax.broadcasted_iota(jnp.int32, (SUBLANES, seq), 0)
    blocks = []
    for hh in range(2):
        rem = sg_ref[pl.ds(2 * hp + hh, 1), :]
        blk = jnp.zeros((SUBLANES, seq), F32)
        for p in range(BIAS_PIECES):
            piece = rem.astype(BF16).astype(F32)
            blk = jnp.where(row == p, piece, blk)
            rem = rem - piece
        blocks.append(blk)
    pad = jnp.zeros((half - SUBLANES, seq), F32)
    pieces = jnp.concatenate([blocks[1], pad, blocks[0], pad], axis=0)
    lane = lax.broadcasted_iota(jnp.int32, (tk, LANES), 1)

    def lane_set(lo, hi):
        return jnp.where((lane >= lo) & (lane < hi), 1.0, 0.0).astype(BF16)

    own = (lane_set(0, half), lane_set(half, 2 * half))
    bias_lanes = (lane_set(half, half + BIAS_PIECES), lane_set(0, BIAS_PIECES))
    for c in range(seq // tk):
        sl = slice(c * tk, (c + 1) * tk)
        bias_t = pieces[:, sl].T.astype(BF16)
        k = k_ref[0, sl, :]
        q = q_ref[0, sl, :]
        for hh in range(2):
            ka_scr[hh, sl, :] = k * (1.0 - bias_lanes[hh]) + bias_t * bias_lanes[hh]
            qa_scr[hh, sl, :] = q * own[hh] + bias_lanes[hh]
        vt = v_ref[0, sl, :].astype(F32).T
        vt_scr[0, 0:half, sl] = vt[0:half].astype(BF16)
        vt_scr[1, 0:half, sl] = vt[half:2 * half].astype(BF16)
    ones_rows = jnp.where(lax.broadcasted_iota(jnp.int32, (VT_ROWS - half, seq), 0) == 0, 1.0, 0.0).astype(BF16)
    vt_scr[0, half:VT_ROWS, :] = ones_rows
    vt_scr[1, half:VT_ROWS, :] = ones_rows

    nt = (((1,), (1,)), ((), ()))
    hq = tq // 2
    tri = lax.broadcasted_iota(jnp.int32, (hq, hq), 0) <= lax.broadcasted_iota(jnp.int32, (hq, hq), 1)
    for i in range(seq // tq):
        lo, mid, n = i * tq, i * tq + hq, (i + 1) * tq
        outs = []
        for hh in range(2):
            qa = qa_scr[hh, lo:n, :]
            if lo > 0:
                s_scr[hh, 0:lo, :] = lax.dot_general(ka_scr[hh, 0:lo, :], qa, nt, preferred_element_type=F32)
            d1 = lax.dot_general(ka_scr[hh, lo:mid, :], qa, nt, preferred_element_type=F32)
            s_scr[hh, lo:mid, 0:hq] = jnp.where(tri, d1[:, 0:hq], -jnp.inf)
            s_scr[hh, lo:mid, hq:tq] = d1[:, hq:tq]
            d2 = lax.dot_general(ka_scr[hh, mid:n, :], qa[hq:tq], nt, preferred_element_type=F32)
            d2 = jnp.where(tri, d2, -jnp.inf)
            s_scr[hh, mid:n, hq:tq] = d2
            slab = jnp.max(s_scr[hh, 0:mid, :].reshape(mid // MAX_SLAB, MAX_SLAB, tq), axis=0)
            m = jnp.max(slab, axis=0, keepdims=True)
            m_right = jnp.maximum(m[:, hq:tq], jnp.max(d2, axis=0, keepdims=True))
            m = jnp.concatenate([m[:, 0:hq], m_right], axis=1)
            acc = None
            for c0 in range(0, mid, PV_CHUNK):
                c1 = min(c0 + PV_CHUNK, mid)
                p_scr[hh, c0:c1, :] = jnp.exp(s_scr[hh, c0:c1, :] - m).astype(BF16)
                part = jnp.dot(vt_scr[hh, :, c0:c1], p_scr[hh, c0:c1, :], preferred_element_type=F32)
                acc = part if acc is None else acc + part
            p_scr[hh, mid:n, hq:tq] = jnp.exp(s_scr[hh, mid:n, hq:tq] - m_right).astype(BF16)
            right = jnp.dot(vt_scr[hh, :, mid:n], p_scr[hh, mid:n, hq:tq], preferred_element_type=F32)
            acc = jnp.concatenate([acc[:, 0:hq], acc[:, hq:tq] + right], axis=1)
            outs.append(acc[0:half] * (1.0 / acc[half:half + 1]))
        o_ref[0, lo:n, :] = jnp.concatenate(outs, axis=0).T.astype(BF16)


def _fox_attention(qkv, sg, batch, seq):
    pairs = FOX_HEADS // 2

    def blk(off):
        return pl.BlockSpec((1, seq, LANES), lambda b, p: (off + p, b, 0))

    return pl.pallas_call(
        functools.partial(_fox_kernel, seq=seq),
        grid=(batch, pairs),
        in_specs=[blk(0), blk(pairs), blk(2 * pairs), pl.BlockSpec((SG_ROWS, seq), lambda b, p: (0, b))],
        out_specs=blk(0),
        out_shape=jax.ShapeDtypeStruct((pairs, batch * seq, LANES), BF16),
        scratch_shapes=[pltpu.VMEM((2, seq, LANES), BF16),
                        pltpu.VMEM((2, seq, LANES), BF16),
                        pltpu.VMEM((2, VT_ROWS, seq), BF16),
                        pltpu.VMEM((2, seq, ATT_TQ), F32),
                        pltpu.VMEM((2, seq, ATT_TQ), BF16)],
        compiler_params=pltpu.CompilerParams(dimension_semantics=("arbitrary", "arbitrary"),
                                             vmem_limit_bytes=VMEM_LIMIT),
        name="fox_attn",
    )(qkv, qkv, qkv, sg)


def _mlstm_kernel(qk_ref, v_ref, mo_ref, sg_ref, o_ref, c_scr, *, seq):
    chains = MLSTM_SEQS * MLSTM_HEADS
    lc = MLSTM_CHUNK
    dk = MLSTM_HEAD_DIM
    t_idx = lax.broadcasted_iota(jnp.int32, (lc, lc), 0)
    s_idx = lax.broadcasted_iota(jnp.int32, (lc, lc), 1)
    causal = s_idx <= t_idx
    c_scr[...] = jnp.zeros(c_scr.shape, F32)

    def chunk(c, carry):
        r0 = pl.multiple_of(c * lc, lc)
        rows = pl.ds(r0, lc)
        new_carry = []
        for ch in range(chains):
            sq, hd = divmod(ch, MLSTM_HEADS)
            n_row, m_prev = carry[ch]
            cmat = c_scr[ch]
            q = qk_ref[sq, rows, hd * dk:(hd + 1) * dk]
            k = qk_ref[sq, rows, MLSTM_WIDTH + hd * dk:MLSTM_WIDTH + (hd + 1) * dk]
            v = v_ref[sq, rows, hd * dk:(hd + 1) * dk]
            gate_cols = pl.ds(sq * seq + r0, lc)
            i_row = sg_ref[SG_I + hd:SG_I + hd + 1, gate_cols]
            lf_row = sg_ref[SG_LOGF + hd:SG_LOGF + hd + 1, gate_cols]
            b_row = sg_ref[SG_B + hd:SG_B + hd + 1, gate_cols]
            b_col = jnp.sum(jnp.where(causal, lf_row, 0.0), axis=-1, keepdims=True)
            g = jnp.sum(lf_row, axis=-1, keepdims=True)
            a_row = i_row - b_row
            dmat = jnp.where(causal, b_col + a_row, -jnp.inf)
            m_inter = b_col + m_prev
            m_t = jnp.maximum(jnp.max(dmat, axis=-1, keepdims=True), m_inter)
            kt = k.astype(F32).T
            s = jnp.dot(q, kt.astype(BF16), preferred_element_type=F32) * jnp.exp(dmat - m_t)
            inter = jnp.exp(m_inter - m_t)
            num = (jnp.dot(s.astype(BF16), v, preferred_element_type=F32)
                   + inter * jnp.dot(q, cmat.astype(BF16), preferred_element_type=F32))
            qn = jnp.sum(q.astype(F32) * n_row, axis=-1, keepdims=True)
            den = jnp.sum(s, axis=-1, keepdims=True) + inter * qn
            h = num * (1.0 / jnp.maximum(jnp.abs(den), jnp.exp(-m_t)))
            gate = _sigmoid(mo_ref[sq, rows, hd * dk:(hd + 1) * dk].astype(F32))
            o_ref[sq, rows, hd * dk:(hd + 1) * dk] = (gate * h).astype(BF16)

            kdec = g + a_row
            m_new = jnp.maximum(g + m_prev, jnp.max(kdec, axis=-1, keepdims=True))
            wk = jnp.exp(kdec - m_new)
            cdec = jnp.exp(g + m_prev - m_new)
            kwt = (kt * wk).astype(BF16)
            c_scr[ch] = cdec * cmat + jnp.dot(kwt, v, preferred_element_type=F32)
            wk8 = jnp.broadcast_to(wk, (SUBLANES, lc)).astype(BF16)
            n_new = cdec * n_row + jnp.dot(wk8, k, preferred_element_type=F32)[0:1, :]
            new_carry.append((n_new, m_new))
        return tuple(new_carry)

    init = tuple((jnp.zeros((1, dk), F32), jnp.zeros((1, 1), F32)) for _ in range(chains))
    lax.fori_loop(0, seq // lc, chunk, init)


def _mlstm(mqk, mv, mo, sg, batch, seq):
    mqk3 = mqk.reshape(batch, seq, 2 * MLSTM_WIDTH)
    mv3 = mv.reshape(batch, seq, MLSTM_WIDTH)
    mo3 = mo.reshape(batch, seq, MLSTM_WIDTH)

    def seq_blk(w):
        return pl.BlockSpec((MLSTM_SEQS, seq, w), lambda b: (b, 0, 0))

    return pl.pallas_call(
        functools.partial(_mlstm_kernel, seq=seq),
        grid=(batch // MLSTM_SEQS,),
        in_specs=[seq_blk(2 * MLSTM_WIDTH), seq_blk(MLSTM_WIDTH), seq_blk(MLSTM_WIDTH),
                  pl.BlockSpec((SG_ROWS, MLSTM_SEQS * seq), lambda b: (0, b))],
        out_specs=seq_blk(MLSTM_WIDTH),
        out_shape=jax.ShapeDtypeStruct((batch, seq, MLSTM_WIDTH), BF16),
        scratch_shapes=[pltpu.VMEM((MLSTM_SEQS * MLSTM_HEADS, MLSTM_HEAD_DIM, MLSTM_HEAD_DIM), F32)],
        compiler_params=pltpu.CompilerParams(dimension_semantics=("arbitrary",),
                                             vmem_limit_bytes=VMEM_LIMIT),
        name="mlstm",
    )(mqk3, mv3, mo3, sg)


def _merge_ffn_kernel(x_ref, a_ref, hm_ref, gates_ref, wuf_ref, wum_ref, wo_ref, g_ref, win_ref, wout_ref,
                      fg_ref, o_ref, mg_scr, x2_scr, h_scr, a_scr, *, final_norm):
    att = jnp.concatenate([a_ref[p] for p in range(FOX_WIDTH // LANES)], axis=1)
    for c in range(D_MODEL // COL_CHUNK):
        lo = c * COL_CHUNK
        a = jnp.dot(att, wuf_ref[:, lo:lo + COL_CHUNK], preferred_element_type=F32)
        bm = jnp.dot(hm_ref[...], wum_ref[:, lo:lo + COL_CHUNK], preferred_element_type=F32)
        ga = gates_ref[:, lo:lo + COL_CHUNK].astype(F32)
        gb = gates_ref[:, D_MODEL + lo:D_MODEL + lo + COL_CHUNK].astype(F32)
        mg_scr[:, lo:lo + COL_CHUNK] = (_sigmoid(ga) * a + _sigmoid(gb) * bm).astype(BF16)
    for c in range(D_MODEL // COL_CHUNK):
        lo = c * COL_CHUNK
        y = jnp.dot(mg_scr[...], wo_ref[:, lo:lo + COL_CHUNK], preferred_element_type=F32)
        x2_scr[:, lo:lo + COL_CHUNK] = x_ref[:, lo:lo + COL_CHUNK] + y
    h_scr[...] = _rms(x2_scr[...], g_ref[...]).astype(BF16)
    _swiglu_into(h_scr, a_scr, win_ref)
    for c in range(D_MODEL // COL_CHUNK):
        lo = c * COL_CHUNK
        y = jnp.dot(a_scr[...], wout_ref[:, lo:lo + COL_CHUNK], preferred_element_type=F32)
        o_ref[:, lo:lo + COL_CHUNK] = x2_scr[:, lo:lo + COL_CHUNK] + 0.5 * y
    if final_norm:
        o_ref[...] = _rms(o_ref[...], fg_ref[...])


def _merge_ffn(x, a, hm, gates, wuf, wum, wo, g, w_in, w_out, fg, final_norm):
    t = x.shape[0]

    def row(w):
        return pl.BlockSpec((ROW_TILE, w), lambda i: (i, 0))

    return pl.pallas_call(
        functools.partial(_merge_ffn_kernel, final_norm=final_norm),
        grid=(t // ROW_TILE,),
        in_specs=[row(D_MODEL), pl.BlockSpec((FOX_WIDTH // LANES, ROW_TILE, LANES), lambda i: (0, i, 0)),
                  row(MLSTM_WIDTH), row(2 * D_MODEL),
                  _resident(wuf.shape), _resident(wum.shape), _resident(wo.shape), _resident((1, D_MODEL)),
                  _resident(w_in.shape), _resident(w_out.shape), _resident((1, D_MODEL))],
        out_specs=row(D_MODEL),
        out_shape=jax.ShapeDtypeStruct((t, D_MODEL), F32),
        scratch_shapes=[pltpu.VMEM((ROW_TILE, D_MODEL), BF16), pltpu.VMEM((ROW_TILE, D_MODEL), F32),
                        pltpu.VMEM((ROW_TILE, D_MODEL), BF16), pltpu.VMEM((ROW_TILE, D_FF), BF16)],
        compiler_params=pltpu.CompilerParams(dimension_semantics=("arbitrary",),
                                             vmem_limit_bytes=VMEM_LIMIT),
        name="merge_ffn",
    )(x, a, hm, gates, wuf, wum, wo, g, w_in, w_out, fg)


def kernel(x, ffn1_norm, ffn1_w_in, ffn1_w_out, mix_norm, w_in, fox_f_bias, mlstm_conv_w, mlstm_conv_b,
           mlstm_i_bias, mlstm_f_bias, w_up_fox, w_up_mlstm, w_out, ffn2_norm, ffn2_w_in, ffn2_w_out,
           final_norm):
    batch, seq, d = x.shape
    depth = ffn1_norm.shape[0]
    assert depth >= 1
    assert d == D_MODEL and seq % ROW_TILE == 0 and seq % ATT_TQ == 0 and seq % MLSTM_CHUNK == 0
    assert ROW_TILE % MLSTM_CHUNK == 0 and mlstm_conv_w.shape[1] == CONV_WIDTH and batch % MLSTM_SEQS == 0
    xf = x.reshape(batch * seq, d)
    fg = final_norm.reshape(1, d)

    o_ff = 3 * FOX_WIDTH
    o_mqk = o_ff + FOX_HEADS
    o_mv = o_mqk + 2 * MLSTM_WIDTH
    o_mi = o_mv + MLSTM_WIDTH
    o_mf = o_mi + MLSTM_HEADS
    o_mo = o_mf + MLSTM_HEADS
    o_g = o_mo + MLSTM_WIDTH

    for l in range(depth):
        w = w_in[l]
        wa = jnp.concatenate([w[:, :FOX_WIDTH] * FOX_HEAD_DIM ** -0.5, w[:, FOX_WIDTH:o_ff]], axis=1).astype(BF16)
        wm = jnp.concatenate([w[:, o_mqk:o_mi], w[:, o_mo:o_g]], axis=1).astype(BF16)
        wg = w[:, o_g:].astype(BF16)
        ws = jnp.concatenate([w[:, o_ff:o_mqk], w[:, o_mi:o_mo]], axis=1).T.astype(BF16)
        bs = jnp.concatenate([fox_f_bias[l], mlstm_i_bias[l], mlstm_f_bias[l]]).reshape(N_SMALL, 1).astype(F32)

        xf = _ffn(xf, ffn1_norm[l].reshape(1, d), ffn1_w_in[l].astype(BF16), ffn1_w_out[l].astype(BF16))
        qkv, mqk, mv, mo, gates, sg = _inproj(
            xf, mix_norm[l].reshape(1, d), wa, wm, wg, ws, bs, mlstm_conv_w[l].astype(F32),
            mlstm_conv_b[l].reshape(1, -1).astype(F32), seq)
        att = _fox_attention(qkv, sg, batch, seq)
        hm = _mlstm(mqk, mv, mo, sg, batch, seq).reshape(batch * seq, MLSTM_WIDTH)
        xf = _merge_ffn(xf, att, hm, gates, w_up_fox[l].astype(BF16), w_up_mlstm[l].astype(BF16),
                        w_out[l].astype(BF16), ffn2_norm[l].reshape(1, d), ffn2_w_in[l].astype(BF16),
                        ffn2_w_out[l].astype(BF16), fg, final_norm=(l == depth - 1))
    return xf.reshape(batch, seq, d)
```

```python
import functools

import jax
import jax.numpy as jnp
from jax import lax
from jax.experimental import pallas as pl
from jax.experimental.pallas import tpu as pltpu

D_MODEL = 1024
FOX_HEADS = 8
FOX_HEAD_DIM = 64
FOX_WIDTH = FOX_HEADS * FOX_HEAD_DIM
MLSTM_HEADS = 4
MLSTM_HEAD_DIM = 128
MLSTM_WIDTH = MLSTM_HEADS * MLSTM_HEAD_DIM
CONV_WIDTH = 4
D_FF = 2816
RMS_EPS = 1e-6
N_SMALL = 16
SG_ROWS = 24
SG_I = FOX_HEADS
SG_LOGF = FOX_HEADS + MLSTM_HEADS
SG_B = N_SMALL + MLSTM_HEADS

V7X_VMEM_BYTES = 64 * 1024 * 1024
LANES = 128
SUBLANES = 8

ROW_TILE = 512
FF_CHUNK = 256
COL_CHUNK = 512
ATT_TQ = 1024
MLSTM_CHUNK = 256
MLSTM_SEQS = 2
VMEM_LIMIT = V7X_VMEM_BYTES * 7 // 8

F32 = jnp.float32
BF16 = jnp.bfloat16


def _rms(x, g):
    ms = jnp.mean(x * x, axis=-1, keepdims=True)
    return x * lax.rsqrt(ms + RMS_EPS) * g


def _sigmoid(x):
    return 1.0 / (1.0 + jnp.exp(-x))


def _log_sigmoid(x):
    return jnp.minimum(x, 0.0) - jnp.log1p(jnp.exp(-jnp.abs(x)))


def _resident(shape):
    nd = len(shape)
    return pl.BlockSpec(shape, lambda *_: (0,) * nd, pipeline_mode=pl.Buffered(1))


def _cumsum_lanes(x, segment=None):
    n = x.shape[-1]
    segment = n if segment is None else segment
    pos = lax.broadcasted_iota(jnp.int32, x.shape, x.ndim - 1) & (segment - 1)
    k = 1
    while k < segment:
        x = x + jnp.where(pos >= k, pltpu.roll(x, k, axis=x.ndim - 1), 0.0)
        k *= 2
    return x


def _swiglu_into(h_scr, a_scr, win_ref):
    for f in range(D_FF // FF_CHUNK):
        lo = f * FF_CHUNK
        h = h_scr[...]
        gate = jnp.dot(h, win_ref[:, lo:lo + FF_CHUNK], preferred_element_type=F32)
        up = jnp.dot(h, win_ref[:, D_FF + lo:D_FF + lo + FF_CHUNK], preferred_element_type=F32)
        a_scr[:, lo:lo + FF_CHUNK] = (gate * _sigmoid(gate) * up).astype(BF16)


def _ffn_kernel(x_ref, g_ref, win_ref, wout_ref, o_ref, h_scr, a_scr):
    h_scr[...] = _rms(x_ref[...], g_ref[...]).astype(BF16)
    _swiglu_into(h_scr, a_scr, win_ref)
    for c in range(D_MODEL // COL_CHUNK):
        lo = c * COL_CHUNK
        y = jnp.dot(a_scr[...], wout_ref[:, lo:lo + COL_CHUNK], preferred_element_type=F32)
        o_ref[:, lo:lo + COL_CHUNK] = x_ref[:, lo:lo + COL_CHUNK] + 0.5 * y


def _ffn(x, g, w_in, w_out):
    t = x.shape[0]
    row = pl.BlockSpec((ROW_TILE, D_MODEL), lambda i: (i, 0))
    return pl.pallas_call(
        _ffn_kernel,
        grid=(t // ROW_TILE,),
        in_specs=[row, _resident((1, D_MODEL)), _resident(w_in.shape), _resident(w_out.shape)],
        out_specs=row,
        out_shape=jax.ShapeDtypeStruct((t, D_MODEL), F32),
        scratch_shapes=[pltpu.VMEM((ROW_TILE, D_MODEL), BF16), pltpu.VMEM((ROW_TILE, D_FF), BF16)],
        compiler_params=pltpu.CompilerParams(dimension_semantics=("arbitrary",),
                                             vmem_limit_bytes=VMEM_LIMIT),
        name="ffn",
    )(x, g, w_in, w_out)


def _inproj_kernel(x_ref, g_ref, wa_ref, wm_ref, wg_ref, ws_ref, bs_ref, cw_ref, cb_ref,
                   qkv_ref, mqk_ref, mv_ref, mo_ref, gates_ref, sg_ref, h_scr, xpad, ccar, *,
                   tiles_per_seq, n_tiles):
    i = pl.program_id(0)
    slot = lax.rem(i, 2)
    prev = 1 - slot
    live = i < n_tiles

    @pl.when(i == 0)
    def _():
        xpad[1] = jnp.zeros(xpad.shape[1:], F32)

    @pl.when(i % tiles_per_seq == 0)
    def _():
        xpad[slot, 0:SUBLANES, :] = jnp.zeros((SUBLANES, 2 * MLSTM_WIDTH), F32)

    @pl.when((i % tiles_per_seq == 0) & live)
    def _():
        ccar[1] = jnp.zeros(ccar.shape[1:], F32)

    @pl.when(i % tiles_per_seq != 0)
    def _():
        xpad[slot, 0:SUBLANES, :] = xpad[prev, ROW_TILE:ROW_TILE + SUBLANES, :]

    h_scr[...] = _rms(x_ref[...], g_ref[...]).astype(BF16)

    sg = lax.dot_general(ws_ref[...], h_scr[...], (((1,), (1,)), ((), ())),
                         preferred_element_type=F32) + bs_ref[...]
    r = lax.broadcasted_iota(jnp.int32, sg.shape, 0)
    is_i_gate = (r >= FOX_HEADS) & (r < FOX_HEADS + MLSTM_HEADS)
    sg = jnp.where(is_i_gate, sg, _log_sigmoid(sg))
    carry_in = jnp.where(live, ccar[1], ccar[0])
    c_fox = _cumsum_lanes(sg[0:FOX_HEADS]) + carry_in[:, 0:1]
    ccar[0] = carry_in
    ccar[1] = jnp.broadcast_to(c_fox[:, ROW_TILE - 1:ROW_TILE], carry_in.shape)
    sg_ref[0:FOX_HEADS, :] = -c_fox
    sg_ref[FOX_HEADS:N_SMALL, :] = sg[FOX_HEADS:N_SMALL]
    sg_ref[N_SMALL:SG_ROWS, :] = _cumsum_lanes(sg[FOX_HEADS:N_SMALL], segment=MLSTM_CHUNK)

    def proj(w_ref, lo):
        return jnp.dot(h_scr[...], w_ref[:, lo:lo + COL_CHUNK], preferred_element_type=F32)

    qscale = MLSTM_HEAD_DIM ** -0.5

    def conv_group(lo, after):
        w = [cw_ref[j:j + 1, lo:lo + LANES] for j in range(CONV_WIDTH)]
        xv = xpad[prev, :, lo:lo + LANES]
        x1 = pltpu.roll(xv, 1, axis=0)
        u = w[3] * xv + w[2] * x1
        v = w[1] * xv + w[0] * x1
        y = (u + pltpu.roll(v, 2, axis=0))[SUBLANES:, :] + (cb_ref[:, lo:lo + LANES] + after)
        y = y * _sigmoid(y)
        if lo < MLSTM_WIDTH:
            y = y * qscale
        mqk_ref[:, lo:lo + LANES] = y.astype(BF16)

    def zero_of(res):
        return jnp.minimum(jnp.abs(res[0:SUBLANES, 0:LANES]), 0.0)[0:1, :]

    def store_cols(o_ref, lo):
        def store(res):
            o_ref[:, lo:lo + COL_CHUNK] = res.astype(BF16)
        return store

    def store_qkv(c):
        def store(res):
            for j in range(COL_CHUNK // LANES):
                qkv_ref[c * (COL_CHUNK // LANES) + j] = res[:, j * LANES:(j + 1) * LANES].astype(BF16)
        return store

    groups = list(range(0, 2 * MLSTM_WIDTH, LANES))
    for c in range(2 * MLSTM_WIDTH // COL_CHUNK):
        xpad[slot, SUBLANES:SUBLANES + ROW_TILE, c * COL_CHUNK:(c + 1) * COL_CHUNK] = proj(wm_ref, c * COL_CHUNK)
    others = [(store_cols(mv_ref, 0), wm_ref, 2 * MLSTM_WIDTH), (store_cols(mo_ref, 0), wm_ref, 3 * MLSTM_WIDTH)]
    others += [(store_qkv(c), wa_ref, c * COL_CHUNK) for c in range(3 * FOX_WIDTH // COL_CHUNK)]
    others += [(store_cols(gates_ref, c * COL_CHUNK), wg_ref, c * COL_CHUNK)
               for c in range(2 * D_MODEL // COL_CHUNK)]
    assert len(others) >= len(groups)
    for idx, (store, w_ref, w_lo) in enumerate(others):
        res = proj(w_ref, w_lo)
        store(res)
        if idx < len(groups):
            conv_group(groups[idx], zero_of(res))


def _inproj(x, g, wa, wm, wg, ws, bs, cw, cb, seq):
    t = x.shape[0]
    nt = t // ROW_TILE

    def row(w):
        return pl.BlockSpec((ROW_TILE, w), lambda i: (jnp.minimum(i, nt - 1), 0))

    qkv_slabs = 3 * FOX_WIDTH // LANES
    out_shape = (
        jax.ShapeDtypeStruct((qkv_slabs, t, LANES), BF16),
        jax.ShapeDtypeStruct((t, 2 * MLSTM_WIDTH), BF16),
        jax.ShapeDtypeStruct((t, MLSTM_WIDTH), BF16),
        jax.ShapeDtypeStruct((t, MLSTM_WIDTH), BF16),
        jax.ShapeDtypeStruct((t, 2 * D_MODEL), BF16),
        jax.ShapeDtypeStruct((SG_ROWS, t), F32),
    )
    return pl.pallas_call(
        functools.partial(_inproj_kernel, tiles_per_seq=seq // ROW_TILE, n_tiles=nt),
        grid=(nt + 1,),
        in_specs=[row(D_MODEL), _resident((1, D_MODEL)), _resident(wa.shape), _resident(wm.shape),
                  _resident(wg.shape), _resident(ws.shape), _resident(bs.shape), _resident(cw.shape),
                  _resident(cb.shape)],
        out_specs=(pl.BlockSpec((qkv_slabs, ROW_TILE, LANES), lambda i: (0, jnp.minimum(i, nt - 1), 0)),
                   pl.BlockSpec((ROW_TILE, 2 * MLSTM_WIDTH), lambda i: (jnp.maximum(i - 1, 0), 0)),
                   row(MLSTM_WIDTH), row(MLSTM_WIDTH), row(2 * D_MODEL),
                   pl.BlockSpec((SG_ROWS, ROW_TILE), lambda i: (0, jnp.minimum(i, nt - 1)))),
        out_shape=out_shape,
        scratch_shapes=[pltpu.VMEM((ROW_TILE, D_MODEL), BF16),
                        pltpu.VMEM((2, ROW_TILE + SUBLANES, 2 * MLSTM_WIDTH), F32),
                        pltpu.VMEM((2, FOX_HEADS, LANES), F32)],
        compiler_params=pltpu.CompilerParams(dimension_semantics=("arbitrary",),
                                             vmem_limit_bytes=VMEM_LIMIT),
        name="inproj",
    )(x, g, wa, wm, wg, ws, bs, cw, cb)


BIAS_PIECES = 3
VT_ROWS = FOX_HEAD_DIM + 16
MAX_SLAB = 64
ATT_STAGE = 256
PV_CHUNK = 512


def _fox_kernel(q_ref, k_ref, v_ref, sg_ref, o_ref, qa_scr, ka_scr, vt_scr, s_scr, p_scr, *, seq):
    hp = pl.program_id(1)
    tq = ATT_TQ
    tk = ATT_STAGE
    half = FOX_HEAD_DIM

    row = lax.broadcasted_iota(jnp.int32, (SUBLANES, seq), 0)
    blocks = []
    for hh in range(2):
        rem = sg_ref[pl.ds(2 * hp + hh, 1), :]
        blk = jnp.zeros((SUBLANES, seq), F32)
        for p in range(BIAS_PIECES):
            piece = rem.astype(BF16).astype(F32)
            blk = jnp.where(row == p, piece, blk)
            rem = rem - piece
        blocks.append(blk)
    pad = jnp.zeros((half - SUBLANES, seq), F32)
    pieces = jnp.concatenate([blocks[1], pad, blocks[0], pad], axis=0)
    lane = lax.broadcasted_iota(jnp.int32, (tk, LANES), 1)

    def lane_set(lo, hi):
        return jnp.where((lane >= lo) & (lane < hi), 1.0, 0.0).astype(BF16)

    own = (lane_set(0, half), lane_set(half, 2 * half))
    bias_lanes = (lane_set(half, half + BIAS_PIECES), lane_set(0, BIAS_PIECES))
    for c in range(seq // tk):
        sl = slice(c * tk, (c + 1) * tk)
        bias_t = pieces[:, sl].T.astype(BF16)
        k = k_ref[0, sl, :]
        q = q_ref[0, sl, :]
        for hh in range(2):
            ka_scr[hh, sl, :] = k * (1.0 - bias_lanes[hh]) + bias_t * bias_lanes[hh]
            qa_scr[hh, sl, :] = q * own[hh] + bias_lanes[hh]
        vt = v_ref[0, sl, :].astype(F32).T
        vt_scr[0, 0:half, sl] = vt[0:half].astype(BF16)
        vt_scr[1, 0:half, sl] = vt[half:2 * half].astype(BF16)
    ones_rows = jnp.where(lax.broadcasted_iota(jnp.int32, (VT_ROWS - half, seq), 0) == 0, 1.0, 0.0).astype(BF16)
    vt_scr[0, half:VT_ROWS, :] = ones_rows
    vt_scr[1, half:VT_ROWS, :] = ones_rows

    nt = (((1,), (1,)), ((), ()))
    hq = tq // 2
    tri = lax.broadcasted_iota(jnp.int32, (hq, hq), 0) <= lax.broadcasted_iota(jnp.int32, (hq, hq), 1)
    for i in range(seq // tq):
        lo, mid, n = i * tq, i * tq + hq, (i + 1) * tq
        outs = []
        for hh in range(2):
            qa = qa_scr[hh, lo:n, :]
            if lo > 0:
                s_scr[hh, 0:lo, :] = lax.dot_general(ka_scr[hh, 0:lo, :], qa, nt, preferred_element_type=F32)
            d1 = lax.dot_general(ka_scr[hh, lo:mid, :], qa, nt, preferred_element_type=F32)
            s_scr[hh, lo:mid, 0:hq] = jnp.where(tri, d1[:, 0:hq], -jnp.inf)
            s_scr[hh, lo:mid, hq:tq] = d1[:, hq:tq]
            d2 = lax.dot_general(ka_scr[hh, mid:n, :], qa[hq:tq], nt, preferred_element_type=F32)
            d2 = jnp.where(tri, d2, -jnp.inf)
            s_scr[hh, mid:n, hq:tq] = d2
            slab = jnp.max(s_scr[hh, 0:mid, :].reshape(mid // MAX_SLAB, MAX_SLAB, tq), axis=0)
            m = jnp.max(slab, axis=0, keepdims=True)
            m_right = jnp.maximum(m[:, hq:tq], jnp.max(d2, axis=0, keepdims=True))
            m = jnp.concatenate([m[:, 0:hq], m_right], axis=1)
            acc = None
            for c0 in range(0, mid, PV_CHUNK):
                c1 = min(c0 + PV_CHUNK, mid)
                p_scr[hh, c0:c1, :] = jnp.exp(s_scr[hh, c0:c1, :] - m).astype(BF16)
                part = jnp.dot(vt_scr[hh, :, c0:c1], p_scr[hh, c0:c1, :], preferred_element_type=F32)
                acc = part if acc is None else acc + part
            p_scr[hh, mid:n, hq:tq] = jnp.exp(s_scr[hh, mid:n, hq:tq] - m_right).astype(BF16)
            right = jnp.dot(vt_scr[hh, :, mid:n], p_scr[hh, mid:n, hq:tq], preferred_element_type=F32)
            acc = jnp.concatenate([acc[:, 0:hq], acc[:, hq:tq] + right], axis=1)
            outs.append(acc[0:half] * (1.0 / acc[half:half + 1]))
        o_ref[0, lo:n, :] = jnp.concatenate(outs, axis=0).T.astype(BF16)


def _fox_attention(qkv, sg, batch, seq):
    pairs = FOX_HEADS // 2

    def blk(off):
        return pl.BlockSpec((1, seq, LANES), lambda b, p: (off + p, b, 0))

    return pl.pallas_call(
        functools.partial(_fox_kernel, seq=seq),
        grid=(batch, pairs),
        in_specs=[blk(0), blk(pairs), blk(2 * pairs), pl.BlockSpec((SG_ROWS, seq), lambda b, p: (0, b))],
        out_specs=blk(0),
        out_shape=jax.ShapeDtypeStruct((pairs, batch * seq, LANES), BF16),
        scratch_shapes=[pltpu.VMEM((2, seq, LANES), BF16),
                        pltpu.VMEM((2, seq, LANES), BF16),
                        pltpu.VMEM((2, VT_ROWS, seq), BF16),
                        pltpu.VMEM((2, seq, ATT_TQ), F32),
                        pltpu.VMEM((2, seq, ATT_TQ), BF16)],
        compiler_params=pltpu.CompilerParams(dimension_semantics=("arbitrary", "arbitrary"),
                                             vmem_limit_bytes=VMEM_LIMIT),
        name="fox_attn",
    )(qkv, qkv, qkv, sg)


def _mlstm_kernel(qk_ref, v_ref, mo_ref, sg_ref, o_ref, c_scr, *, seq):
    chains = MLSTM_SEQS * MLSTM_HEADS
    lc = MLSTM_CHUNK
    dk = MLSTM_HEAD_DIM
    t_idx = lax.broadcasted_iota(jnp.int32, (lc, lc), 0)
    s_idx = lax.broadcasted_iota(jnp.int32, (lc, lc), 1)
    causal = s_idx <= t_idx
    c_scr[...] = jnp.zeros(c_scr.shape, F32)

    def chunk(c, carry):
        r0 = pl.multiple_of(c * lc, lc)
        rows = pl.ds(r0, lc)
        new_carry = []
        for ch in range(chains):
            sq, hd = divmod(ch, MLSTM_HEADS)
            n_row, m_prev = carry[ch]
            cmat = c_scr[ch]
            q = qk_ref[sq, rows, hd * dk:(hd + 1) * dk]
            k = qk_ref[sq, rows, MLSTM_WIDTH + hd * dk:MLSTM_WIDTH + (hd + 1) * dk]
            v = v_ref[sq, rows, hd * dk:(hd + 1) * dk]
            gate_cols = pl.ds(sq * seq + r0, lc)
            i_row = sg_ref[SG_I + hd:SG_I + hd + 1, gate_cols]
            lf_row = sg_ref[SG_LOGF + hd:SG_LOGF + hd + 1, gate_cols]
            b_row = sg_ref[SG_B + hd:SG_B + hd + 1, gate_cols]
            b_col = jnp.sum(jnp.where(causal, lf_row, 0.0), axis=-1, keepdims=True)
            g = jnp.sum(lf_row, axis=-1, keepdims=True)
            a_row = i_row - b_row
            dmat = jnp.where(causal, b_col + a_row, -jnp.inf)
            m_inter = b_col + m_prev
            m_t = jnp.maximum(jnp.max(dmat, axis=-1, keepdims=True), m_inter)
            kt = k.astype(F32).T
            s = jnp.dot(q, kt.astype(BF16), preferred_element_type=F32) * jnp.exp(dmat - m_t)
            inter = jnp.exp(m_inter - m_t)
            num = (jnp.dot(s.astype(BF16), v, preferred_element_type=F32)
                   + inter * jnp.dot(q, cmat.astype(BF16), preferred_element_type=F32))
            qn = jnp.sum(q.astype(F32) * n_row, axis=-1, keepdims=True)
            den = jnp.sum(s, axis=-1, keepdims=True) + inter * qn
            h = num * (1.0 / jnp.maximum(jnp.abs(den), jnp.exp(-m_t)))
            gate = _sigmoid(mo_ref[sq, rows, hd * dk:(hd + 1) * dk].astype(F32))
            o_ref[sq, rows, hd * dk:(hd + 1) * dk] = (gate * h).astype(BF16)

            kdec = g + a_row
            m_new = jnp.maximum(g + m_prev, jnp.max(kdec, axis=-1, keepdims=True))
            wk = jnp.exp(kdec - m_new)
            cdec = jnp.exp(g + m_prev - m_new)
            kwt = (kt * wk).astype(BF16)
            c_scr[ch] = cdec * cmat + jnp.dot(kwt, v, preferred_element_type=F32)
            wk8 = jnp.broadcast_to(wk, (SUBLANES, lc)).astype(BF16)
            n_new = cdec * n_row + jnp.dot(wk8, k, preferred_element_type=F32)[0:1, :]
            new_carry.append((n_new, m_new))
        return tuple(new_carry)

    init = tuple((jnp.zeros((1, dk), F32), jnp.zeros((1, 1), F32)) for _ in range(chains))
    lax.fori_loop(0, seq // lc, chunk, init)


def _mlstm(mqk, mv, mo, sg, batch, seq):
    mqk3 = mqk.reshape(batch, seq, 2 * MLSTM_WIDTH)
    mv3 = mv.reshape(batch, seq, MLSTM_WIDTH)
    mo3 = mo.reshape(batch, seq, MLSTM_WIDTH)

    def seq_blk(w):
        return pl.BlockSpec((MLSTM_SEQS, seq, w), lambda b: (b, 0, 0))

    return pl.pallas_call(
        functools.partial(_mlstm_kernel, seq=seq),
        grid=(batch // MLSTM_SEQS,),
        in_specs=[seq_blk(2 * MLSTM_WIDTH), seq_blk(MLSTM_WIDTH), seq_blk(MLSTM_WIDTH),
                  pl.BlockSpec((SG_ROWS, MLSTM_SEQS * seq), lambda b: (0, b))],
        out_specs=seq_blk(MLSTM_WIDTH),
        out_shape=jax.ShapeDtypeStruct((batch, seq, MLSTM_WIDTH), BF16),
        scratch_shapes=[pltpu.VMEM((MLSTM_SEQS * MLSTM_HEADS, MLSTM_HEAD_DIM, MLSTM_HEAD_DIM), F32)],
        compiler_params=pltpu.CompilerParams(dimension_semantics=("arbitrary",),
                                             vmem_limit_bytes=VMEM_LIMIT),
        name="mlstm",
    )(mqk3, mv3, mo3, sg)


def _merge_ffn_kernel(x_ref, a_ref, hm_ref, gates_ref, wuf_ref, wum_ref, wo_ref, g_ref, win_ref, wout_ref,
                      fg_ref, o_ref, mg_scr, x2_scr, h_scr, a_scr, *, final_norm):
    att = jnp.concatenate([a_ref[p] for p in range(FOX_WIDTH // LANES)], axis=1)
    for c in range(D_MODEL // COL_CHUNK):
        lo = c * COL_CHUNK
        a = jnp.dot(att, wuf_ref[:, lo:lo + COL_CHUNK], preferred_element_type=F32)
        bm = jnp.dot(hm_ref[...], wum_ref[:, lo:lo + COL_CHUNK], preferred_element_type=F32)
        ga = gates_ref[:, lo:lo + COL_CHUNK].astype(F32)
        gb = gates_ref[:, D_MODEL + lo:D_MODEL + lo + COL_CHUNK].astype(F32)
        mg_scr[:, lo:lo + COL_CHUNK] = (_sigmoid(ga) * a + _sigmoid(gb) * bm).astype(BF16)
    for c in range(D_MODEL // COL_CHUNK):
        lo = c * COL_CHUNK
        y = jnp.dot(mg_scr[...], wo_ref[:, lo:lo + COL_CHUNK], preferred_element_type=F32)
        x2_scr[:, lo:lo + COL_CHUNK] = x_ref[:, lo:lo + COL_CHUNK] + y
    h_scr[...] = _rms(x2_scr[...], g_ref[...]).astype(BF16)
    _swiglu_into(h_scr, a_scr, win_ref)
    for c in range(D_MODEL // COL_CHUNK):
        lo = c * COL_CHUNK
        y = jnp.dot(a_scr[...], wout_ref[:, lo:lo + COL_CHUNK], preferred_element_type=F32)
        o_ref[:, lo:lo + COL_CHUNK] = x2_scr[:, lo:lo + COL_CHUNK] + 0.5 * y
    if final_norm:
        o_ref[...] = _rms(o_ref[...], fg_ref[...])


def _merge_ffn(x, a, hm, gates, wuf, wum, wo, g, w_in, w_out, fg, final_norm):
    t = x.shape[0]

    def row(w):
        return pl.BlockSpec((ROW_TILE, w), lambda i: (i, 0))

    return pl.pallas_call(
        functools.partial(_merge_ffn_kernel, final_norm=final_norm),
        grid=(t // ROW_TILE,),
        in_specs=[row(D_MODEL), pl.BlockSpec((FOX_WIDTH // LANES, ROW_TILE, LANES), lambda i: (0, i, 0)),
                  row(MLSTM_WIDTH), row(2 * D_MODEL),
                  _resident(wuf.shape), _resident(wum.shape), _resident(wo.shape), _resident((1, D_MODEL)),
                  _resident(w_in.shape), _resident(w_out.shape), _resident((1, D_MODEL))],
        out_specs=row(D_MODEL),
        out_shape=jax.ShapeDtypeStruct((t, D_MODEL), F32),
        scratch_shapes=[pltpu.VMEM((ROW_TILE, D_MODEL), BF16), pltpu.VMEM((ROW_TILE, D_MODEL), F32),
                        pltpu.VMEM((ROW_TILE, D_MODEL), BF16), pltpu.VMEM((ROW_TILE, D_FF), BF16)],
        compiler_params=pltpu.CompilerParams(dimension_semantics=("arbitrary",),
                                             vmem_limit_bytes=VMEM_LIMIT),
        name="merge_ffn",
    )(x, a, hm, gates, wuf, wum, wo, g, w_in, w_out, fg)


def kernel(x, ffn1_norm, ffn1_w_in, ffn1_w_out, mix_norm, w_in, fox_f_bias, mlstm_conv_w, mlstm_conv_b,
           mlstm_i_bias, mlstm_f_bias, w_up_fox, w_up_mlstm, w_out, ffn2_norm, ffn2_w_in, ffn2_w_out,
           final_norm):
    batch, seq, d = x.shape
    depth = ffn1_norm.shape[0]
    assert depth >= 1
    assert d == D_MODEL and seq % ROW_TILE == 0 and seq % ATT_TQ == 0 and seq % MLSTM_CHUNK == 0
    assert ROW_TILE % MLSTM_CHUNK == 0 and mlstm_conv_w.shape[1] == CONV_WIDTH and batch % MLSTM_SEQS == 0
    xf = x.reshape(batch * seq, d)
    fg = final_norm.reshape(1, d)

    o_ff = 3 * FOX_WIDTH
    o_mqk = o_ff + FOX_HEADS
    o_mv = o_mqk + 2 * MLSTM_WIDTH
    o_mi = o_mv + MLSTM_WIDTH
    o_mf = o_mi + MLSTM_HEADS
    o_mo = o_mf + MLSTM_HEADS
    o_g = o_mo + MLSTM_WIDTH

    for l in range(depth):
        w = w_in[l]
        wa = jnp.concatenate([w[:, :FOX_WIDTH] * FOX_HEAD_DIM ** -0.5, w[:, FOX_WIDTH:o_ff]], axis=1).astype(BF16)
        wm = jnp.concatenate([w[:, o_mqk:o_mi], w[:, o_mo:o_g]], axis=1).astype(BF16)
        wg = w[:, o_g:].astype(BF16)
        ws = jnp.concatenate([w[:, o_ff:o_mqk], w[:, o_mi:o_mo]], axis=1).T.astype(BF16)
        bs = jnp.concatenate([fox_f_bias[l], mlstm_i_bias[l], mlstm_f_bias[l]]).reshape(N_SMALL, 1).astype(F32)

        xf = _ffn(xf, ffn1_norm[l].reshape(1, d), ffn1_w_in[l].astype(BF16), ffn1_w_out[l].astype(BF16))
        qkv, mqk, mv, mo, gates, sg = _inproj(
            xf, mix_norm[l].reshape(1, d), wa, wm, wg, ws, bs, mlstm_conv_w[l].astype(F32),
            mlstm_conv_b[l].reshape(1, -1).astype(F32), seq)
        att = _fox_attention(qkv, sg, batch, seq)
        hm = _mlstm(mqk, mv, mo, sg, batch, seq).reshape(batch * seq, MLSTM_WIDTH)
        xf = _merge_ffn(xf, att, hm, gates, w_up_fox[l].astype(BF16), w_up_mlstm[l].astype(BF16),
                        w_out[l].astype(BF16), ffn2_norm[l].reshape(1, d), ffn2_w_in[l].astype(BF16),
                        ffn2_w_out[l].astype(BF16), fg, final_norm=(l == depth - 1))
    return xf.reshape(batch, seq, d)
```

```python
import functools

import jax
import jax.numpy as jnp
from jax import lax
from jax.experimental import pallas as pl
from jax.experimental.pallas import tpu as pltpu

D_MODEL = 1024
FOX_HEADS = 8
FOX_HEAD_DIM = 64
FOX_WIDTH = FOX_HEADS * FOX_HEAD_DIM
MLSTM_HEADS = 4
MLSTM_HEAD_DIM = 128
MLSTM_WIDTH = MLSTM_HEADS * MLSTM_HEAD_DIM
CONV_WIDTH = 4
D_FF = 2816
RMS_EPS = 1e-6
N_SMALL = 16
SG_ROWS = 24
SG_I = FOX_HEADS
SG_LOGF = FOX_HEADS + MLSTM_HEADS
SG_B = N_SMALL + MLSTM_HEADS

V7X_VMEM_BYTES = 64 * 1024 * 1024
LANES = 128
SUBLANES = 8

ROW_TILE = 512
FF_CHUNK = 256
COL_CHUNK = 512
ATT_TQ = 1024
MLSTM_CHUNK = 256
MLSTM_SEQS = 2
VMEM_LIMIT = V7X_VMEM_BYTES * 7 // 8

F32 = jnp.float32
BF16 = jnp.bfloat16


def _rms(x, g):
    ms = jnp.mean(x * x, axis=-1, keepdims=True)
    return x * lax.rsqrt(ms + RMS_EPS) * g


def _sigmoid(x):
    return 1.0 / (1.0 + jnp.exp(-x))


def _log_sigmoid(x):
    return jnp.minimum(x, 0.0) - jnp.log1p(jnp.exp(-jnp.abs(x)))


def _resident(shape):
    nd = len(shape)
    return pl.BlockSpec(shape, lambda *_: (0,) * nd, pipeline_mode=pl.Buffered(1))


def _cumsum_lanes(x, segment=None):
    n = x.shape[-1]
    segment = n if segment is None else segment
    pos = lax.broadcasted_iota(jnp.int32, x.shape, x.ndim - 1) & (segment - 1)
    k = 1
    while k < segment:
        x = x + jnp.where(pos >= k, pltpu.roll(x, k, axis=x.ndim - 1), 0.0)
        k *= 2
    return x


def _swiglu_into(h_scr, a_scr, win_ref):
    for f in range(D_FF // FF_CHUNK):
        lo = f * FF_CHUNK
        h = h_scr[...]
        gate = jnp.dot(h, win_ref[:, lo:lo + FF_CHUNK], preferred_element_type=F32)
        up = jnp.dot(h, win_ref[:, D_FF + lo:D_FF + lo + FF_CHUNK], preferred_element_type=F32)
        a_scr[:, lo:lo + FF_CHUNK] = (gate * _sigmoid(gate) * up).astype(BF16)


def _ffn_kernel(x_ref, g_ref, win_ref, wout_ref, o_ref, h_scr, a_scr):
    h_scr[...] = _rms(x_ref[...], g_ref[...]).astype(BF16)
    _swiglu_into(h_scr, a_scr, win_ref)
    for c in range(D_MODEL // COL_CHUNK):
        lo = c * COL_CHUNK
        y = jnp.dot(a_scr[...], wout_ref[:, lo:lo + COL_CHUNK], preferred_element_type=F32)
        o_ref[:, lo:lo + COL_CHUNK] = x_ref[:, lo:lo + COL_CHUNK] + 0.5 * y


def _ffn(x, g, w_in, w_out):
    t = x.shape[0]
    row = pl.BlockSpec((ROW_TILE, D_MODEL), lambda i: (i, 0))
    return pl.pallas_call(
        _ffn_kernel,
        grid=(t // ROW_TILE,),
        in_specs=[row, _resident((1, D_MODEL)), _resident(w_in.shape), _resident(w_out.shape)],
        out_specs=row,
        out_shape=jax.ShapeDtypeStruct((t, D_MODEL), F32),
        scratch_shapes=[pltpu.VMEM((ROW_TILE, D_MODEL), BF16), pltpu.VMEM((ROW_TILE, D_FF), BF16)],
        compiler_params=pltpu.CompilerParams(dimension_semantics=("arbitrary",),
                                             vmem_limit_bytes=VMEM_LIMIT),
        name="ffn",
    )(x, g, w_in, w_out)


def _inproj_kernel(x_ref, g_ref, wa_ref, wm_ref, wg_ref, ws_ref, bs_ref, cw_ref, cb_ref,
                   qkv_ref, mqk_ref, mv_ref, mo_ref, gates_ref, sg_ref, h_scr, xpad, ccar, *,
                   tiles_per_seq, n_tiles):
    i = pl.program_id(0)
    slot = lax.rem(i, 2)
    prev = 1 - slot
    live = i < n_tiles

    @pl.when(i == 0)
    def _():
        xpad[1] = jnp.zeros(xpad.shape[1:], F32)

    @pl.when(i % tiles_per_seq == 0)
    def _():
        xpad[slot, 0:SUBLANES, :] = jnp.zeros((SUBLANES, 2 * MLSTM_WIDTH), F32)

    @pl.when((i % tiles_per_seq == 0) & live)
    def _():
        ccar[1] = jnp.zeros(ccar.shape[1:], F32)

    @pl.when(i % tiles_per_seq != 0)
    def _():
        xpad[slot, 0:SUBLANES, :] = xpad[prev, ROW_TILE:ROW_TILE + SUBLANES, :]

    h_scr[...] = _rms(x_ref[...], g_ref[...]).astype(BF16)

    sg = lax.dot_general(ws_ref[...], h_scr[...], (((1,), (1,)), ((), ())),
                         preferred_element_type=F32) + bs_ref[...]
    r = lax.broadcasted_iota(jnp.int32, sg.shape, 0)
    is_i_gate = (r >= FOX_HEADS) & (r < FOX_HEADS + MLSTM_HEADS)
    sg = jnp.where(is_i_gate, sg, _log_sigmoid(sg))
    carry_in = jnp.where(live, ccar[1], ccar[0])
    c_fox = _cumsum_lanes(sg[0:FOX_HEADS]) + carry_in[:, 0:1]
    ccar[0] = carry_in
    ccar[1] = jnp.broadcast_to(c_fox[:, ROW_TILE - 1:ROW_TILE], carry_in.shape)
    sg_ref[0:FOX_HEADS, :] = -c_fox
    sg_ref[FOX_HEADS:N_SMALL, :] = sg[FOX_HEADS:N_SMALL]
    sg_ref[N_SMALL:SG_ROWS, :] = _cumsum_lanes(sg[FOX_HEADS:N_SMALL], segment=MLSTM_CHUNK)

    def proj(w_ref, lo):
        return jnp.dot(h_scr[...], w_ref[:, lo:lo + COL_CHUNK], preferred_element_type=F32)

    qscale = MLSTM_HEAD_DIM ** -0.5

    def conv_group(lo, after):
        w = [cw_ref[j:j + 1, lo:lo + LANES] for j in range(CONV_WIDTH)]
        xv = xpad[prev, :, lo:lo + LANES]
        x1 = pltpu.roll(xv, 1, axis=0)
        u = w[3] * xv + w[2] * x1
        v = w[1] * xv + w[0] * x1
        y = (u + pltpu.roll(v, 2, axis=0))[SUBLANES:, :] + (cb_ref[:, lo:lo + LANES] + after)
        y = y * _sigmoid(y)
        if lo < MLSTM_WIDTH:
            y = y * qscale
        mqk_ref[:, lo:lo + LANES] = y.astype(BF16)

    def zero_of(res):
        return jnp.minimum(jnp.abs(res[0:SUBLANES, 0:LANES]), 0.0)[0:1, :]

    def store_cols(o_ref, lo):
        def store(res):
            o_ref[:, lo:lo + COL_CHUNK] = res.astype(BF16)
        return store

    def store_qkv(c):
        def store(res):
            for j in range(COL_CHUNK // LANES):
                qkv_ref[c * (COL_CHUNK // LANES) + j] = res[:, j * LANES:(j + 1) * LANES].astype(BF16)
        return store

    groups = list(range(0, 2 * MLSTM_WIDTH, LANES))
    for c in range(2 * MLSTM_WIDTH // COL_CHUNK):
        xpad[slot, SUBLANES:SUBLANES + ROW_TILE, c * COL_CHUNK:(c + 1) * COL_CHUNK] = proj(wm_ref, c * COL_CHUNK)
    others = [(store_cols(mv_ref, 0), wm_ref, 2 * MLSTM_WIDTH), (store_cols(mo_ref, 0), wm_ref, 3 * MLSTM_WIDTH)]
    others += [(store_qkv(c), wa_ref, c * COL_CHUNK) for c in range(3 * FOX_WIDTH // COL_CHUNK)]
    others += [(store_cols(gates_ref, c * COL_CHUNK), wg_ref, c * COL_CHUNK)
               for c in range(2 * D_MODEL // COL_CHUNK)]
    assert len(others) >= len(groups)
    for idx, (store, w_ref, w_lo) in enumerate(others):
        res = proj(w_ref, w_lo)
        store(res)
        if idx < len(groups):
            conv_group(groups[idx], zero_of(res))


def _inproj(x, g, wa, wm, wg, ws, bs, cw, cb, seq):
    t = x.shape[0]
    nt = t // ROW_TILE

    def row(w):
        return pl.BlockSpec((ROW_TILE, w), lambda i: (jnp.minimum(i, nt - 1), 0))

    qkv_slabs = 3 * FOX_WIDTH // LANES
    out_shape = (
        jax.ShapeDtypeStruct((qkv_slabs, t, LANES), BF16),
        jax.ShapeDtypeStruct((t, 2 * MLSTM_WIDTH), BF16),
        jax.ShapeDtypeStruct((t, MLSTM_WIDTH), BF16),
        jax.ShapeDtypeStruct((t, MLSTM_WIDTH), BF16),
        jax.ShapeDtypeStruct((t, 2 * D_MODEL), BF16),
        jax.ShapeDtypeStruct((SG_ROWS, t), F32),
    )
    return pl.pallas_call(
        functools.partial(_inproj_kernel, tiles_per_seq=seq // ROW_TILE, n_tiles=nt),
        grid=(nt + 1,),
        in_specs=[row(D_MODEL), _resident((1, D_MODEL)), _resident(wa.shape), _resident(wm.shape),
                  _resident(wg.shape), _resident(ws.shape), _resident(bs.shape), _resident(cw.shape),
                  _resident(cb.shape)],
        out_specs=(pl.BlockSpec((qkv_slabs, ROW_TILE, LANES), lambda i: (0, jnp.minimum(i, nt - 1), 0)),
                   pl.BlockSpec((ROW_TILE, 2 * MLSTM_WIDTH), lambda i: (jnp.maximum(i - 1, 0), 0)),
                   row(MLSTM_WIDTH), row(MLSTM_WIDTH), row(2 * D_MODEL),
                   pl.BlockSpec((SG_ROWS, ROW_TILE), lambda i: (0, jnp.minimum(i, nt - 1)))),
        out_shape=out_shape,
        scratch_shapes=[pltpu.VMEM((ROW_TILE, D_MODEL), BF16),
                        pltpu.VMEM((2, ROW_TILE + SUBLANES, 2 * MLSTM_WIDTH), F32),
                        pltpu.VMEM((2, FOX_HEADS, LANES), F32)],
        compiler_params=pltpu.CompilerParams(dimension_semantics=("arbitrary",),
                                             vmem_limit_bytes=VMEM_LIMIT),
        name="inproj",
    )(x, g, wa, wm, wg, ws, bs, cw, cb)


BIAS_PIECES = 3
VT_ROWS = FOX_HEAD_DIM + 16
MAX_SLAB = 64
ATT_STAGE = 256
PV_CHUNK = 256


def _fox_kernel(q_ref, k_ref, v_ref, sg_ref, o_ref, qa_scr, ka_scr, vt_scr, s_scr, p_scr, *, seq):
    hp = pl.program_id(1)
    tq = ATT_TQ
    tk = ATT_STAGE
    half = FOX_HEAD_DIM

    row = lax.broadcasted_iota(jnp.int32, (SUBLANES, seq), 0)
    blocks = []
    for hh in range(2):
        rem = sg_ref[pl.ds(2 * hp + hh, 1), :]
        blk = jnp.zeros((SUBLANES, seq), F32)
        for p in range(BIAS_PIECES):
            piece = rem.astype(BF16).astype(F32)
            blk = jnp.where(row == p, piece, blk)
            rem = rem - piece
        blocks.append(blk)
    pad = jnp.zeros((half - SUBLANES, seq), F32)
    pieces = jnp.concatenate([blocks[1], pad, blocks[0], pad], axis=0)
    lane = lax.broadcasted_iota(jnp.int32, (tk, LANES), 1)

    def lane_set(lo, hi):
        return jnp.where((lane >= lo) & (lane < hi), 1.0, 0.0).astype(BF16)

    own = (lane_set(0, half), lane_set(half, 2 * half))
    bias_lanes = (lane_set(half, half + BIAS_PIECES), lane_set(0, BIAS_PIECES))
    for c in range(seq // tk):
        sl = slice(c * tk, (c + 1) * tk)
        bias_t = pieces[:, sl].T.astype(BF16)
        k = k_ref[0, sl, :]
        q = q_ref[0, sl, :]
        for hh in range(2):
            ka_scr[hh, sl, :] = k * (1.0 - bias_lanes[hh]) + bias_t * bias_lanes[hh]
            qa_scr[hh, sl, :] = q * own[hh] + bias_lanes[hh]
        vt = v_ref[0, sl, :].astype(F32).T
        vt_scr[0, 0:half, sl] = vt[0:half].astype(BF16)
        vt_scr[1, 0:half, sl] = vt[half:2 * half].astype(BF16)
    ones_rows = jnp.where(lax.broadcasted_iota(jnp.int32, (VT_ROWS - half, seq), 0) == 0, 1.0, 0.0).astype(BF16)
    vt_scr[0, half:VT_ROWS, :] = ones_rows
    vt_scr[1, half:VT_ROWS, :] = ones_rows

    nt = (((1,), (1,)), ((), ()))
    hq = tq // 2
    tri = lax.broadcasted_iota(jnp.int32, (hq, hq), 0) <= lax.broadcasted_iota(jnp.int32, (hq, hq), 1)

    def zero_of(x):
        return jnp.minimum(jnp.abs(x[0:1, 0:1]), 0.0)

    def stage_a(i, hh):
        lo, mid, n = i * tq, i * tq + hq, (i + 1) * tq
        qa = qa_scr[hh, lo:n, :]
        products = []
        if lo > 0:
            past = lax.dot_general(ka_scr[hh, 0:lo, :], qa, nt, preferred_element_type=F32)
            s_scr[hh, 0:lo, :] = past
            products.append(past)
        d1 = lax.dot_general(ka_scr[hh, lo:mid, :], qa, nt, preferred_element_type=F32)
        s_scr[hh, lo:mid, 0:hq] = jnp.where(tri, d1[:, 0:hq], -jnp.inf)
        s_scr[hh, lo:mid, hq:tq] = d1[:, hq:tq]
        products.append(d1)
        d2 = lax.dot_general(ka_scr[hh, mid:n, :], qa[hq:tq], nt, preferred_element_type=F32)
        d2 = jnp.where(tri, d2, -jnp.inf)
        s_scr[hh, mid:n, hq:tq] = d2
        slab = jnp.max(s_scr[hh, 0:mid, :].reshape(mid // MAX_SLAB, MAX_SLAB, tq), axis=0)
        m = jnp.max(slab, axis=0, keepdims=True)
        m_right = jnp.maximum(m[:, hq:tq], jnp.max(d2, axis=0, keepdims=True))
        m = jnp.concatenate([m[:, 0:hq], m_right], axis=1)
        scores = jnp.concatenate(products, axis=0) if len(products) > 1 else products[0]
        return m, m_right, scores

    def stage_b(i, hh, m, m_right, next_scores):
        lo, mid, n = i * tq, i * tq + hq, (i + 1) * tq
        starts = list(range(0, mid, PV_CHUNK))
        acc = None
        for idx, c0 in enumerate(starts):
            c1 = min(c0 + PV_CHUNK, mid)
            mc = m
            if next_scores is not None:
                r = (next_scores.shape[0] * (idx + 1) // (len(starts) + 1)) // SUBLANES * SUBLANES
                mc = m + zero_of(next_scores[r:r + SUBLANES, :])
            p_scr[hh, c0:c1, :] = jnp.exp(s_scr[hh, c0:c1, :] - mc).astype(BF16)
            part = jnp.dot(vt_scr[hh, :, c0:c1], p_scr[hh, c0:c1, :], preferred_element_type=F32)
            acc = part if acc is None else acc + part
        p_scr[hh, mid:n, hq:tq] = jnp.exp(s_scr[hh, mid:n, hq:tq] - m_right).astype(BF16)
        right = jnp.dot(vt_scr[hh, :, mid:n], p_scr[hh, mid:n, hq:tq], preferred_element_type=F32)
        acc = jnp.concatenate([acc[:, 0:hq], acc[:, hq:tq] + right], axis=1)
        return acc[0:half] * (1.0 / acc[half:half + 1])

    units = [(i, hh) for i in range(seq // tq) for hh in range(2)]
    a_now = stage_a(*units[0])
    outs = {}
    for u, (i, hh) in enumerate(units):
        a_next = stage_a(*units[u + 1]) if u + 1 < len(units) else None
        m, m_right, _ = a_now
        outs[hh] = stage_b(i, hh, m, m_right, None if a_next is None else a_next[2])
        a_now = a_next
        if hh == 1:
            o_ref[0, i * tq:(i + 1) * tq, :] = jnp.concatenate([outs[0], outs[1]], axis=0).T.astype(BF16)


def _fox_attention(qkv, sg, batch, seq):
    pairs = FOX_HEADS // 2

    def blk(off):
        return pl.BlockSpec((1, seq, LANES), lambda b, p: (off + p, b, 0))

    return pl.pallas_call(
        functools.partial(_fox_kernel, seq=seq),
        grid=(batch, pairs),
        in_specs=[blk(0), blk(pairs), blk(2 * pairs), pl.BlockSpec((SG_ROWS, seq), lambda b, p: (0, b))],
        out_specs=blk(0),
        out_shape=jax.ShapeDtypeStruct((pairs, batch * seq, LANES), BF16),
        scratch_shapes=[pltpu.VMEM((2, seq, LANES), BF16),
                        pltpu.VMEM((2, seq, LANES), BF16),
                        pltpu.VMEM((2, VT_ROWS, seq), BF16),
                        pltpu.VMEM((2, seq, ATT_TQ), F32),
                        pltpu.VMEM((2, seq, ATT_TQ), BF16)],
        compiler_params=pltpu.CompilerParams(dimension_semantics=("arbitrary", "arbitrary"),
                                             vmem_limit_bytes=VMEM_LIMIT),
        name="fox_attn",
    )(qkv, qkv, qkv, sg)


def _mlstm_kernel(qk_ref, v_ref, mo_ref, sg_ref, o_ref, c_scr, *, seq):
    chains = MLSTM_SEQS * MLSTM_HEADS
    lc = MLSTM_CHUNK
    dk = MLSTM_HEAD_DIM
    t_idx = lax.broadcasted_iota(jnp.int32, (lc, lc), 0)
    s_idx = lax.broadcasted_iota(jnp.int32, (lc, lc), 1)
    causal = s_idx <= t_idx
    c_scr[...] = jnp.zeros(c_scr.shape, F32)

    def chunk(c, carry):
        r0 = pl.multiple_of(c * lc, lc)
        rows = pl.ds(r0, lc)
        new_carry = []
        for ch in range(chains):
            sq, hd = divmod(ch, MLSTM_HEADS)
            n_row, m_prev = carry[ch]
            cmat = c_scr[ch]
            q = qk_ref[sq, rows, hd * dk:(hd + 1) * dk]
            k = qk_ref[sq, rows, MLSTM_WIDTH + hd * dk:MLSTM_WIDTH + (hd + 1) * dk]
            v = v_ref[sq, rows, hd * dk:(hd + 1) * dk]
            gate_cols = pl.ds(sq * seq + r0, lc)
            i_row = sg_ref[SG_I + hd:SG_I + hd + 1, gate_cols]
            lf_row = sg_ref[SG_LOGF + hd:SG_LOGF + hd + 1, gate_cols]
            b_row = sg_ref[SG_B + hd:SG_B + hd + 1, gate_cols]
            b_col = jnp.sum(jnp.where(causal, lf_row, 0.0), axis=-1, keepdims=True)
            g = jnp.sum(lf_row, axis=-1, keepdims=True)
            a_row = i_row - b_row
            dmat = jnp.where(causal, b_col + a_row, -jnp.inf)
            m_inter = b_col + m_prev
            m_t = jnp.maximum(jnp.max(dmat, axis=-1, keepdims=True), m_inter)
            kt = k.astype(F32).T
            s = jnp.dot(q, kt.astype(BF16), preferred_element_type=F32) * jnp.exp(dmat - m_t)
            inter = jnp.exp(m_inter - m_t)
            num = (jnp.dot(s.astype(BF16), v, preferred_element_type=F32)
                   + inter * jnp.dot(q, cmat.astype(BF16), preferred_element_type=F32))
            qn = jnp.sum(q.astype(F32) * n_row, axis=-1, keepdims=True)
            den = jnp.sum(s, axis=-1, keepdims=True) + inter * qn
            h = num * (1.0 / jnp.maximum(jnp.abs(den), jnp.exp(-m_t)))
            gate = _sigmoid(mo_ref[sq, rows, hd * dk:(hd + 1) * dk].astype(F32))
            o_ref[sq, rows, hd * dk:(hd + 1) * dk] = (gate * h).astype(BF16)

            kdec = g + a_row
            m_new = jnp.maximum(g + m_prev, jnp.max(kdec, axis=-1, keepdims=True))
            wk = jnp.exp(kdec - m_new)
            cdec = jnp.exp(g + m_prev - m_new)
            kwt = (kt * wk).astype(BF16)
            c_scr[ch] = cdec * cmat + jnp.dot(kwt, v, preferred_element_type=F32)
            wk8 = jnp.broadcast_to(wk, (SUBLANES, lc)).astype(BF16)
            n_new = cdec * n_row + jnp.dot(wk8, k, preferred_element_type=F32)[0:1, :]
            new_carry.append((n_new, m_new))
        return tuple(new_carry)

    init = tuple((jnp.zeros((1, dk), F32), jnp.zeros((1, 1), F32)) for _ in range(chains))
    lax.fori_loop(0, seq // lc, chunk, init)


def _mlstm(mqk, mv, mo, sg, batch, seq):
    mqk3 = mqk.reshape(batch, seq, 2 * MLSTM_WIDTH)
    mv3 = mv.reshape(batch, seq, MLSTM_WIDTH)
    mo3 = mo.reshape(batch, seq, MLSTM_WIDTH)

    def seq_blk(w):
        return pl.BlockSpec((MLSTM_SEQS, seq, w), lambda b: (b, 0, 0))

    return pl.pallas_call(
        functools.partial(_mlstm_kernel, seq=seq),
        grid=(batch // MLSTM_SEQS,),
        in_specs=[seq_blk(2 * MLSTM_WIDTH), seq_blk(MLSTM_WIDTH), seq_blk(MLSTM_WIDTH),
                  pl.BlockSpec((SG_ROWS, MLSTM_SEQS * seq), lambda b: (0, b))],
        out_specs=seq_blk(MLSTM_WIDTH),
        out_shape=jax.ShapeDtypeStruct((batch, seq, MLSTM_WIDTH), BF16),
        scratch_shapes=[pltpu.VMEM((MLSTM_SEQS * MLSTM_HEADS, MLSTM_HEAD_DIM, MLSTM_HEAD_DIM), F32)],
        compiler_params=pltpu.CompilerParams(dimension_semantics=("arbitrary",),
                                             vmem_limit_bytes=VMEM_LIMIT),
        name="mlstm",
    )(mqk3, mv3, mo3, sg)


def _merge_ffn_kernel(x_ref, a_ref, hm_ref, gates_ref, wuf_ref, wum_ref, wo_ref, g_ref, win_ref, wout_ref,
                      fg_ref, o_ref, mg_scr, x2_scr, h_scr, a_scr, *, final_norm):
    att = jnp.concatenate([a_ref[p] for p in range(FOX_WIDTH // LANES)], axis=1)
    for c in range(D_MODEL // COL_CHUNK):
        lo = c * COL_CHUNK
        a = jnp.dot(att, wuf_ref[:, lo:lo + COL_CHUNK], preferred_element_type=F32)
        bm = jnp.dot(hm_ref[...], wum_ref[:, lo:lo + COL_CHUNK], preferred_element_type=F32)
        ga = gates_ref[:, lo:lo + COL_CHUNK].astype(F32)
        gb = gates_ref[:, D_MODEL + lo:D_MODEL + lo + COL_CHUNK].astype(F32)
        mg_scr[:, lo:lo + COL_CHUNK] = (_sigmoid(ga) * a + _sigmoid(gb) * bm).astype(BF16)
    for c in range(D_MODEL // COL_CHUNK):
        lo = c * COL_CHUNK
        y = jnp.dot(mg_scr[...], wo_ref[:, lo:lo + COL_CHUNK], preferred_element_type=F32)
        x2_scr[:, lo:lo + COL_CHUNK] = x_ref[:, lo:lo + COL_CHUNK] + y
    h_scr[...] = _rms(x2_scr[...], g_ref[...]).astype(BF16)
    _swiglu_into(h_scr, a_scr, win_ref)
    for c in range(D_MODEL // COL_CHUNK):
        lo = c * COL_CHUNK
        y = jnp.dot(a_scr[...], wout_ref[:, lo:lo + COL_CHUNK], preferred_element_type=F32)
        o_ref[:, lo:lo + COL_CHUNK] = x2_scr[:, lo:lo + COL_CHUNK] + 0.5 * y
    if final_norm:
        o_ref[...] = _rms(o_ref[...], fg_ref[...])


def _merge_ffn(x, a, hm, gates, wuf, wum, wo, g, w_in, w_out, fg, final_norm):
    t = x.shape[0]

    def row(w):
        return pl.BlockSpec((ROW_TILE, w), lambda i: (i, 0))

    return pl.pallas_call(
        functools.partial(_merge_ffn_kernel, final_norm=final_norm),
        grid=(t // ROW_TILE,),
        in_specs=[row(D_MODEL), pl.BlockSpec((FOX_WIDTH // LANES, ROW_TILE, LANES), lambda i: (0, i, 0)),
                  row(MLSTM_WIDTH), row(2 * D_MODEL),
                  _resident(wuf.shape), _resident(wum.shape), _resident(wo.shape), _resident((1, D_MODEL)),
                  _resident(w_in.shape), _resident(w_out.shape), _resident((1, D_MODEL))],
        out_specs=row(D_MODEL),
        out_shape=jax.ShapeDtypeStruct((t, D_MODEL), F32),
        scratch_shapes=[pltpu.VMEM((ROW_TILE, D_MODEL), BF16), pltpu.VMEM((ROW_TILE, D_MODEL), F32),
                        pltpu.VMEM((ROW_TILE, D_MODEL), BF16), pltpu.VMEM((ROW_TILE, D_FF), BF16)],
        compiler_params=pltpu.CompilerParams(dimension_semantics=("arbitrary",),
                                             vmem_limit_bytes=VMEM_LIMIT),
        name="merge_ffn",
    )(x, a, hm, gates, wuf, wum, wo, g, w_in, w_out, fg)


def kernel(x, ffn1_norm, ffn1_w_in, ffn1_w_out, mix_norm, w_in, fox_f_bias, mlstm_conv_w, mlstm_conv_b,
           mlstm_i_bias, mlstm_f_bias, w_up_fox, w_up_mlstm, w_out, ffn2_norm, ffn2_w_in, ffn2_w_out,
           final_norm):
    batch, seq, d = x.shape
    depth = ffn1_norm.shape[0]
    assert depth >= 1
    assert d == D_MODEL and seq % ROW_TILE == 0 and seq % ATT_TQ == 0 and seq % MLSTM_CHUNK == 0
    assert ROW_TILE % MLSTM_CHUNK == 0 and mlstm_conv_w.shape[1] == CONV_WIDTH and batch % MLSTM_SEQS == 0
    xf = x.reshape(batch * seq, d)
    fg = final_norm.reshape(1, d)

    o_ff = 3 * FOX_WIDTH
    o_mqk = o_ff + FOX_HEADS
    o_mv = o_mqk + 2 * MLSTM_WIDTH
    o_mi = o_mv + MLSTM_WIDTH
    o_mf = o_mi + MLSTM_HEADS
    o_mo = o_mf + MLSTM_HEADS
    o_g = o_mo + MLSTM_WIDTH

    for l in range(depth):
        w = w_in[l]
        wa = jnp.concatenate([w[:, :FOX_WIDTH] * FOX_HEAD_DIM ** -0.5, w[:, FOX_WIDTH:o_ff]], axis=1).astype(BF16)
        wm = jnp.concatenate([w[:, o_mqk:o_mi], w[:, o_mo:o_g]], axis=1).astype(BF16)
        wg = w[:, o_g:].astype(BF16)
        ws = jnp.concatenate([w[:, o_ff:o_mqk], w[:, o_mi:o_mo]], axis=1).T.astype(BF16)
        bs = jnp.concatenate([fox_f_bias[l], mlstm_i_bias[l], mlstm_f_bias[l]]).reshape(N_SMALL, 1).astype(F32)

        xf = _ffn(xf, ffn1_norm[l].reshape(1, d), ffn1_w_in[l].astype(BF16), ffn1_w_out[l].astype(BF16))
        qkv, mqk, mv, mo, gates, sg = _inproj(
            xf, mix_norm[l].reshape(1, d), wa, wm, wg, ws, bs, mlstm_conv_w[l].astype(F32),
            mlstm_conv_b[l].reshape(1, -1).astype(F32), seq)
        att = _fox_attention(qkv, sg, batch, seq)
        hm = _mlstm(mqk, mv, mo, sg, batch, seq).reshape(batch * seq, MLSTM_WIDTH)
        xf = _merge_ffn(xf, att, hm, gates, w_up_fox[l].astype(BF16), w_up_mlstm[l].astype(BF16),
                        w_out[l].astype(BF16), ffn2_norm[l].reshape(1, d), ffn2_w_in[l].astype(BF16),
                        ffn2_w_out[l].astype(BF16), fg, final_norm=(l == depth - 1))
    return xf.reshape(batch, seq, d)
```

```python
import functools

import jax
import jax.numpy as jnp
from jax import lax
from jax.experimental import pallas as pl
from jax.experimental.pallas import tpu as pltpu

D_MODEL = 1024
FOX_HEADS = 8
FOX_HEAD_DIM = 64
FOX_WIDTH = FOX_HEADS * FOX_HEAD_DIM
MLSTM_HEADS = 4
MLSTM_HEAD_DIM = 128
MLSTM_WIDTH = MLSTM_HEADS * MLSTM_HEAD_DIM
CONV_WIDTH = 4
D_FF = 2816
RMS_EPS = 1e-6
N_SMALL = 16
SG_ROWS = 24
SG_I = FOX_HEADS
SG_LOGF = FOX_HEADS + MLSTM_HEADS
SG_B = N_SMALL + MLSTM_HEADS

V7X_VMEM_BYTES = 64 * 1024 * 1024
LANES = 128
SUBLANES = 8

ROW_TILE = 512
FF_CHUNK = 256
COL_CHUNK = 512
ATT_TQ = 1024
MLSTM_CHUNK = 256
MLSTM_SEQS = 2
VMEM_LIMIT = V7X_VMEM_BYTES * 7 // 8

F32 = jnp.float32
BF16 = jnp.bfloat16


def _rms(x, g):
    ms = jnp.mean(x * x, axis=-1, keepdims=True)
    return x * lax.rsqrt(ms + RMS_EPS) * g


def _sigmoid(x):
    return 1.0 / (1.0 + jnp.exp(-x))


def _log_sigmoid(x):
    return jnp.minimum(x, 0.0) - jnp.log1p(jnp.exp(-jnp.abs(x)))


def _resident(shape):
    nd = len(shape)
    return pl.BlockSpec(shape, lambda *_: (0,) * nd, pipeline_mode=pl.Buffered(1))


def _cumsum_lanes(x, segment=None):
    n = x.shape[-1]
    segment = n if segment is None else segment
    pos = lax.broadcasted_iota(jnp.int32, x.shape, x.ndim - 1) & (segment - 1)
    k = 1
    while k < segment:
        x = x + jnp.where(pos >= k, pltpu.roll(x, k, axis=x.ndim - 1), 0.0)
        k *= 2
    return x


def _swiglu_into(h_scr, a_scr, win_ref):
    for f in range(D_FF // FF_CHUNK):
        lo = f * FF_CHUNK
        h = h_scr[...]
        gate = jnp.dot(h, win_ref[:, lo:lo + FF_CHUNK], preferred_element_type=F32)
        up = jnp.dot(h, win_ref[:, D_FF + lo:D_FF + lo + FF_CHUNK], preferred_element_type=F32)
        a_scr[:, lo:lo + FF_CHUNK] = (gate * _sigmoid(gate) * up).astype(BF16)


def _ffn_kernel(x_ref, g_ref, win_ref, wout_ref, o_ref, h_scr, a_scr):
    h_scr[...] = _rms(x_ref[...], g_ref[...]).astype(BF16)
    _swiglu_into(h_scr, a_scr, win_ref)
    for c in range(D_MODEL // COL_CHUNK):
        lo = c * COL_CHUNK
        y = jnp.dot(a_scr[...], wout_ref[:, lo:lo + COL_CHUNK], preferred_element_type=F32)
        o_ref[:, lo:lo + COL_CHUNK] = x_ref[:, lo:lo + COL_CHUNK] + 0.5 * y


def _ffn(x, g, w_in, w_out):
    t = x.shape[0]
    row = pl.BlockSpec((ROW_TILE, D_MODEL), lambda i: (i, 0))
    return pl.pallas_call(
        _ffn_kernel,
        grid=(t // ROW_TILE,),
        in_specs=[row, _resident((1, D_MODEL)), _resident(w_in.shape), _resident(w_out.shape)],
        out_specs=row,
        out_shape=jax.ShapeDtypeStruct((t, D_MODEL), F32),
        scratch_shapes=[pltpu.VMEM((ROW_TILE, D_MODEL), BF16), pltpu.VMEM((ROW_TILE, D_FF), BF16)],
        compiler_params=pltpu.CompilerParams(dimension_semantics=("arbitrary",),
                                             vmem_limit_bytes=VMEM_LIMIT),
        name="ffn",
    )(x, g, w_in, w_out)


def _inproj_kernel(x_ref, g_ref, wa_ref, wm_ref, wg_ref, ws_ref, bs_ref, cw_ref, cb_ref,
                   qkv_ref, mqk_ref, mv_ref, mo_ref, gates_ref, sg_ref, h_scr, xpad, ccar, *,
                   tiles_per_seq, n_tiles):
    i = pl.program_id(0)
    slot = lax.rem(i, 2)
    prev = 1 - slot
    live = i < n_tiles

    @pl.when(i == 0)
    def _():
        xpad[1] = jnp.zeros(xpad.shape[1:], F32)
        h_scr[1] = jnp.zeros(h_scr.shape[1:], BF16)

    @pl.when(i % tiles_per_seq == 0)
    def _():
        xpad[slot, 0:SUBLANES, :] = jnp.zeros((SUBLANES, 2 * MLSTM_WIDTH), F32)

    @pl.when((i % tiles_per_seq == 0) & live)
    def _():
        ccar[1] = jnp.zeros(ccar.shape[1:], F32)

    @pl.when(i % tiles_per_seq != 0)
    def _():
        xpad[slot, 0:SUBLANES, :] = xpad[prev, ROW_TILE:ROW_TILE + SUBLANES, :]

    def proj(h_ref, w_ref, lo):
        return jnp.dot(h_ref[...], w_ref[:, lo:lo + COL_CHUNK], preferred_element_type=F32)

    qscale = MLSTM_HEAD_DIM ** -0.5

    def conv_group(lo, after):
        w = [cw_ref[j:j + 1, lo:lo + LANES] for j in range(CONV_WIDTH)]
        xv = xpad[prev, :, lo:lo + LANES]
        x1 = pltpu.roll(xv, 1, axis=0)
        u = w[3] * xv + w[2] * x1
        v = w[1] * xv + w[0] * x1
        y = (u + pltpu.roll(v, 2, axis=0))[SUBLANES:, :] + (cb_ref[:, lo:lo + LANES] + after)
        y = y * _sigmoid(y)
        if lo < MLSTM_WIDTH:
            y = y * qscale
        mqk_ref[:, lo:lo + LANES] = y.astype(BF16)

    def zero_of(res):
        return jnp.minimum(jnp.abs(res[0:SUBLANES, 0:LANES]), 0.0)[0:1, :]

    def store_cols(o_ref, lo):
        def store(res):
            o_ref[:, lo:lo + COL_CHUNK] = res.astype(BF16)
        return store

    def store_qkv(c):
        def store(res):
            for j in range(COL_CHUNK // LANES):
                qkv_ref[c * (COL_CHUNK // LANES) + j] = res[:, j * LANES:(j + 1) * LANES].astype(BF16)
        return store

    groups = list(range(0, 2 * MLSTM_WIDTH, LANES))
    h_now, h_prev = h_scr.at[slot], h_scr.at[prev]

    lagged = [(store_cols(gates_ref, c * COL_CHUNK), wg_ref, c * COL_CHUNK)
              for c in range(2 * D_MODEL // COL_CHUNK)]
    for idx, (store, w_ref, w_lo) in enumerate(lagged):
        res = proj(h_prev, w_ref, w_lo)
        store(res)
        conv_group(groups[idx], zero_of(res))

    h_scr[slot] = _rms(x_ref[...], g_ref[...]).astype(BF16)

    sg = lax.dot_general(ws_ref[...], h_now[...], (((1,), (1,)), ((), ())),
                         preferred_element_type=F32) + bs_ref[...]
    r = lax.broadcasted_iota(jnp.int32, sg.shape, 0)
    is_i_gate = (r >= FOX_HEADS) & (r < FOX_HEADS + MLSTM_HEADS)
    sg = jnp.where(is_i_gate, sg, _log_sigmoid(sg))
    carry_in = jnp.where(live, ccar[1], ccar[0])
    c_fox = _cumsum_lanes(sg[0:FOX_HEADS]) + carry_in[:, 0:1]
    ccar[0] = carry_in
    ccar[1] = jnp.broadcast_to(c_fox[:, ROW_TILE - 1:ROW_TILE], carry_in.shape)
    sg_ref[0:FOX_HEADS, :] = -c_fox
    sg_ref[FOX_HEADS:N_SMALL, :] = sg[FOX_HEADS:N_SMALL]
    sg_ref[N_SMALL:SG_ROWS, :] = _cumsum_lanes(sg[FOX_HEADS:N_SMALL], segment=MLSTM_CHUNK)

    for c in range(2 * MLSTM_WIDTH // COL_CHUNK):
        xpad[slot, SUBLANES:SUBLANES + ROW_TILE, c * COL_CHUNK:(c + 1) * COL_CHUNK] = proj(
            h_now, wm_ref, c * COL_CHUNK)
    others = [(store_cols(mv_ref, 0), wm_ref, 2 * MLSTM_WIDTH), (store_cols(mo_ref, 0), wm_ref, 3 * MLSTM_WIDTH)]
    others += [(store_qkv(c), wa_ref, c * COL_CHUNK) for c in range(3 * FOX_WIDTH // COL_CHUNK)]
    rest = groups[len(lagged):]
    assert len(others) >= len(rest)
    for idx, (store, w_ref, w_lo) in enumerate(others):
        res = proj(h_now, w_ref, w_lo)
        store(res)
        if idx < len(rest):
            conv_group(rest[idx], zero_of(res))


def _inproj(x, g, wa, wm, wg, ws, bs, cw, cb, seq):
    t = x.shape[0]
    nt = t // ROW_TILE

    def row(w):
        return pl.BlockSpec((ROW_TILE, w), lambda i: (jnp.minimum(i, nt - 1), 0))

    qkv_slabs = 3 * FOX_WIDTH // LANES
    out_shape = (
        jax.ShapeDtypeStruct((qkv_slabs, t, LANES), BF16),
        jax.ShapeDtypeStruct((t, 2 * MLSTM_WIDTH), BF16),
        jax.ShapeDtypeStruct((t, MLSTM_WIDTH), BF16),
        jax.ShapeDtypeStruct((t, MLSTM_WIDTH), BF16),
        jax.ShapeDtypeStruct((t, 2 * D_MODEL), BF16),
        jax.ShapeDtypeStruct((SG_ROWS, t), F32),
    )
    return pl.pallas_call(
        functools.partial(_inproj_kernel, tiles_per_seq=seq // ROW_TILE, n_tiles=nt),
        grid=(nt + 1,),
        in_specs=[row(D_MODEL), _resident((1, D_MODEL)), _resident(wa.shape), _resident(wm.shape),
                  _resident(wg.shape), _resident(ws.shape), _resident(bs.shape), _resident(cw.shape),
                  _resident(cb.shape)],
        out_specs=(pl.BlockSpec((qkv_slabs, ROW_TILE, LANES), lambda i: (0, jnp.minimum(i, nt - 1), 0)),
                   pl.BlockSpec((ROW_TILE, 2 * MLSTM_WIDTH), lambda i: (jnp.maximum(i - 1, 0), 0)),
                   row(MLSTM_WIDTH), row(MLSTM_WIDTH),
                   pl.BlockSpec((ROW_TILE, 2 * D_MODEL), lambda i: (jnp.maximum(i - 1, 0), 0)),
                   pl.BlockSpec((SG_ROWS, ROW_TILE), lambda i: (0, jnp.minimum(i, nt - 1)))),
        out_shape=out_shape,
        scratch_shapes=[pltpu.VMEM((2, ROW_TILE, D_MODEL), BF16),
                        pltpu.VMEM((2, ROW_TILE + SUBLANES, 2 * MLSTM_WIDTH), F32),
                        pltpu.VMEM((2, FOX_HEADS, LANES), F32)],
        compiler_params=pltpu.CompilerParams(dimension_semantics=("arbitrary",),
                                             vmem_limit_bytes=VMEM_LIMIT),
        name="inproj",
    )(x, g, wa, wm, wg, ws, bs, cw, cb)


BIAS_PIECES = 3
VT_ROWS = FOX_HEAD_DIM + 16
MAX_SLAB = 64
ATT_STAGE = 256
PV_CHUNK = 256


def _fox_kernel(q_ref, k_ref, v_ref, sg_ref, o_ref, qa_scr, ka_scr, vt_scr, s_scr, p_scr, *, seq):
    hp = pl.program_id(1)
    tq = ATT_TQ
    tk = ATT_STAGE
    half = FOX_HEAD_DIM

    row = lax.broadcasted_iota(jnp.int32, (SUBLANES, seq), 0)
    blocks = []
    for hh in range(2):
        rem = sg_ref[pl.ds(2 * hp + hh, 1), :]
        blk = jnp.zeros((SUBLANES, seq), F32)
        for p in range(BIAS_PIECES):
            piece = rem.astype(BF16).astype(F32)
            blk = jnp.where(row == p, piece, blk)
            rem = rem - piece
        blocks.append(blk)
    pad = jnp.zeros((half - SUBLANES, seq), F32)
    pieces = jnp.concatenate([blocks[1], pad, blocks[0], pad], axis=0)
    lane = lax.broadcasted_iota(jnp.int32, (tk, LANES), 1)

    def lane_set(lo, hi):
        return jnp.where((lane >= lo) & (lane < hi), 1.0, 0.0).astype(BF16)

    own = (lane_set(0, half), lane_set(half, 2 * half))
    bias_lanes = (lane_set(half, half + BIAS_PIECES), lane_set(0, BIAS_PIECES))
    for c in range(seq // tk):
        sl = slice(c * tk, (c + 1) * tk)
        bias_t = pieces[:, sl].T.astype(BF16)
        k = k_ref[0, sl, :]
        q = q_ref[0, sl, :]
        for hh in range(2):
            ka_scr[hh, sl, :] = k * (1.0 - bias_lanes[hh]) + bias_t * bias_lanes[hh]
            qa_scr[hh, sl, :] = q * own[hh] + bias_lanes[hh]
        vt = v_ref[0, sl, :].astype(F32).T
        vt_scr[0, 0:half, sl] = vt[0:half].astype(BF16)
        vt_scr[1, 0:half, sl] = vt[half:2 * half].astype(BF16)
    ones_rows = jnp.where(lax.broadcasted_iota(jnp.int32, (VT_ROWS - half, seq), 0) == 0, 1.0, 0.0).astype(BF16)
    vt_scr[0, half:VT_ROWS, :] = ones_rows
    vt_scr[1, half:VT_ROWS, :] = ones_rows

    nt = (((1,), (1,)), ((), ()))
    hq = tq // 2
    tri = lax.broadcasted_iota(jnp.int32, (hq, hq), 0) <= lax.broadcasted_iota(jnp.int32, (hq, hq), 1)

    def zero_of(x):
        return jnp.minimum(jnp.abs(x[0:1, 0:1]), 0.0)

    def stage_a(i, hh):
        lo, mid, n = i * tq, i * tq + hq, (i + 1) * tq
        qa = qa_scr[hh, lo:n, :]
        products = []
        if lo > 0:
            past = lax.dot_general(ka_scr[hh, 0:lo, :], qa, nt, preferred_element_type=F32)
            s_scr[hh, 0:lo, :] = past
            products.append(past)
        d1 = lax.dot_general(ka_scr[hh, lo:mid, :], qa, nt, preferred_element_type=F32)
        s_scr[hh, lo:mid, 0:hq] = jnp.where(tri, d1[:, 0:hq], -jnp.inf)
        s_scr[hh, lo:mid, hq:tq] = d1[:, hq:tq]
        products.append(d1)
        d2 = lax.dot_general(ka_scr[hh, mid:n, :], qa[hq:tq], nt, preferred_element_type=F32)
        d2 = jnp.where(tri, d2, -jnp.inf)
        s_scr[hh, mid:n, hq:tq] = d2
        slab = jnp.max(s_scr[hh, 0:mid, :].reshape(mid // MAX_SLAB, MAX_SLAB, tq), axis=0)
        m = jnp.max(slab, axis=0, keepdims=True)
        m_right = jnp.maximum(m[:, hq:tq], jnp.max(d2, axis=0, keepdims=True))
        m = jnp.concatenate([m[:, 0:hq], m_right], axis=1)
        scores = jnp.concatenate(products, axis=0) if len(products) > 1 else products[0]
        return m, m_right, scores

    def stage_b(i, hh, m, m_right, next_scores):
        lo, mid, n = i * tq, i * tq + hq, (i + 1) * tq
        starts = list(range(0, mid, PV_CHUNK))
        acc = None
        for idx, c0 in enumerate(starts):
            c1 = min(c0 + PV_CHUNK, mid)
            mc = m
            if next_scores is not None:
                r = (next_scores.shape[0] * (idx + 1) // (len(starts) + 1)) // SUBLANES * SUBLANES
                mc = m + zero_of(next_scores[r:r + SUBLANES, :])
            p_scr[hh, c0:c1, :] = jnp.exp(s_scr[hh, c0:c1, :] - mc).astype(BF16)
            part = jnp.dot(vt_scr[hh, :, c0:c1], p_scr[hh, c0:c1, :], preferred_element_type=F32)
            acc = part if acc is None else acc + part
        p_scr[hh, mid:n, hq:tq] = jnp.exp(s_scr[hh, mid:n, hq:tq] - m_right).astype(BF16)
        right = jnp.dot(vt_scr[hh, :, mid:n], p_scr[hh, mid:n, hq:tq], preferred_element_type=F32)
        acc = jnp.concatenate([acc[:, 0:hq], acc[:, hq:tq] + right], axis=1)
        return acc[0:half] * (1.0 / acc[half:half + 1])

    units = [(i, hh) for i in range(seq // tq) for hh in range(2)]
    a_now = stage_a(*units[0])
    outs = {}
    for u, (i, hh) in enumerate(units):
        a_next = stage_a(*units[u + 1]) if u + 1 < len(units) else None
        m, m_right, _ = a_now
        outs[hh] = stage_b(i, hh, m, m_right, None if a_next is None else a_next[2])
        a_now = a_next
        if hh == 1:
            o_ref[0, i * tq:(i + 1) * tq, :] = jnp.concatenate([outs[0], outs[1]], axis=0).T.astype(BF16)


def _fox_attention(qkv, sg, batch, seq):
    pairs = FOX_HEADS // 2

    def blk(off):
        return pl.BlockSpec((1, seq, LANES), lambda b, p: (off + p, b, 0))

    return pl.pallas_call(
        functools.partial(_fox_kernel, seq=seq),
        grid=(batch, pairs),
        in_specs=[blk(0), blk(pairs), blk(2 * pairs), pl.BlockSpec((SG_ROWS, seq), lambda b, p: (0, b))],
        out_specs=blk(0),
        out_shape=jax.ShapeDtypeStruct((pairs, batch * seq, LANES), BF16),
        scratch_shapes=[pltpu.VMEM((2, seq, LANES), BF16),
                        pltpu.VMEM((2, seq, LANES), BF16),
                        pltpu.VMEM((2, VT_ROWS, seq), BF16),
                        pltpu.VMEM((2, seq, ATT_TQ), F32),
                        pltpu.VMEM((2, seq, ATT_TQ), BF16)],
        compiler_params=pltpu.CompilerParams(dimension_semantics=("arbitrary", "arbitrary"),
                                             vmem_limit_bytes=VMEM_LIMIT),
        name="fox_attn",
    )(qkv, qkv, qkv, sg)


def _mlstm_kernel(qk_ref, v_ref, mo_ref, sg_ref, o_ref, c_scr, *, seq):
    chains = MLSTM_SEQS * MLSTM_HEADS
    lc = MLSTM_CHUNK
    dk = MLSTM_HEAD_DIM
    t_idx = lax.broadcasted_iota(jnp.int32, (lc, lc), 0)
    s_idx = lax.broadcasted_iota(jnp.int32, (lc, lc), 1)
    causal = s_idx <= t_idx
    c_scr[...] = jnp.zeros(c_scr.shape, F32)

    def chunk(c, carry):
        r0 = pl.multiple_of(c * lc, lc)
        rows = pl.ds(r0, lc)
        new_carry = []
        for ch in range(chains):
            sq, hd = divmod(ch, MLSTM_HEADS)
            n_row, m_prev = carry[ch]
            cmat = c_scr[ch]
            q = qk_ref[sq, rows, hd * dk:(hd + 1) * dk]
            k = qk_ref[sq, rows, MLSTM_WIDTH + hd * dk:MLSTM_WIDTH + (hd + 1) * dk]
            v = v_ref[sq, rows, hd * dk:(hd + 1) * dk]
            gate_cols = pl.ds(sq * seq + r0, lc)
            i_row = sg_ref[SG_I + hd:SG_I + hd + 1, gate_cols]
            lf_row = sg_ref[SG_LOGF + hd:SG_LOGF + hd + 1, gate_cols]
            b_row = sg_ref[SG_B + hd:SG_B + hd + 1, gate_cols]
            b_col = jnp.sum(jnp.where(causal, lf_row, 0.0), axis=-1, keepdims=True)
            g = jnp.sum(lf_row, axis=-1, keepdims=True)
            a_row = i_row - b_row
            dmat = jnp.where(causal, b_col + a_row, -jnp.inf)
            m_inter = b_col + m_prev
            m_t = jnp.maximum(jnp.max(dmat, axis=-1, keepdims=True), m_inter)
            kt = k.astype(F32).T
            s = jnp.dot(q, kt.astype(BF16), preferred_element_type=F32) * jnp.exp(dmat - m_t)
            inter = jnp.exp(m_inter - m_t)
            num = (jnp.dot(s.astype(BF16), v, preferred_element_type=F32)
                   + inter * jnp.dot(q, cmat.astype(BF16), preferred_element_type=F32))
            qn = jnp.sum(q.astype(F32) * n_row, axis=-1, keepdims=True)
            den = jnp.sum(s, axis=-1, keepdims=True) + inter * qn
            h = num * (1.0 / jnp.maximum(jnp.abs(den), jnp.exp(-m_t)))
            gate = _sigmoid(mo_ref[sq, rows, hd * dk:(hd + 1) * dk].astype(F32))
            o_ref[sq, rows, hd * dk:(hd + 1) * dk] = (gate * h).astype(BF16)

            kdec = g + a_row
            m_new = jnp.maximum(g + m_prev, jnp.max(kdec, axis=-1, keepdims=True))
            wk = jnp.exp(kdec - m_new)
            cdec = jnp.exp(g + m_prev - m_new)
            kwt = (kt * wk).astype(BF16)
            c_scr[ch] = cdec * cmat + jnp.dot(kwt, v, preferred_element_type=F32)
            wk8 = jnp.broadcast_to(wk, (SUBLANES, lc)).astype(BF16)
            n_new = cdec * n_row + jnp.dot(wk8, k, preferred_element_type=F32)[0:1, :]
            new_carry.append((n_new, m_new))
        return tuple(new_carry)

    init = tuple((jnp.zeros((1, dk), F32), jnp.zeros((1, 1), F32)) for _ in range(chains))
    lax.fori_loop(0, seq // lc, chunk, init)


def _mlstm(mqk, mv, mo, sg, batch, seq):
    mqk3 = mqk.reshape(batch, seq, 2 * MLSTM_WIDTH)
    mv3 = mv.reshape(batch, seq, MLSTM_WIDTH)
    mo3 = mo.reshape(batch, seq, MLSTM_WIDTH)

    def seq_blk(w):
        return pl.BlockSpec((MLSTM_SEQS, seq, w), lambda b: (b, 0, 0))

    return pl.pallas_call(
        functools.partial(_mlstm_kernel, seq=seq),
        grid=(batch // MLSTM_SEQS,),
        in_specs=[seq_blk(2 * MLSTM_WIDTH), seq_blk(MLSTM_WIDTH), seq_blk(MLSTM_WIDTH),
                  pl.BlockSpec((SG_ROWS, MLSTM_SEQS * seq), lambda b: (0, b))],
        out_specs=seq_blk(MLSTM_WIDTH),
        out_shape=jax.ShapeDtypeStruct((batch, seq, MLSTM_WIDTH), BF16),
        scratch_shapes=[pltpu.VMEM((MLSTM_SEQS * MLSTM_HEADS, MLSTM_HEAD_DIM, MLSTM_HEAD_DIM), F32)],
        compiler_params=pltpu.CompilerParams(dimension_semantics=("arbitrary",),
                                             vmem_limit_bytes=VMEM_LIMIT),
        name="mlstm",
    )(mqk3, mv3, mo3, sg)


def _merge_ffn_kernel(x_ref, a_ref, hm_ref, gates_ref, wuf_ref, wum_ref, wo_ref, g_ref, win_ref, wout_ref,
                      fg_ref, o_ref, mg_scr, x2_scr, h_scr, a_scr, *, final_norm):
    att = jnp.concatenate([a_ref[p] for p in range(FOX_WIDTH // LANES)], axis=1)
    for c in range(D_MODEL // COL_CHUNK):
        lo = c * COL_CHUNK
        a = jnp.dot(att, wuf_ref[:, lo:lo + COL_CHUNK], preferred_element_type=F32)
        bm = jnp.dot(hm_ref[...], wum_ref[:, lo:lo + COL_CHUNK], preferred_element_type=F32)
        ga = gates_ref[:, lo:lo + COL_CHUNK].astype(F32)
        gb = gates_ref[:, D_MODEL + lo:D_MODEL + lo + COL_CHUNK].astype(F32)
        mg_scr[:, lo:lo + COL_CHUNK] = (_sigmoid(ga) * a + _sigmoid(gb) * bm).astype(BF16)
    for c in range(D_MODEL // COL_CHUNK):
        lo = c * COL_CHUNK
        y = jnp.dot(mg_scr[...], wo_ref[:, lo:lo + COL_CHUNK], preferred_element_type=F32)
        x2_scr[:, lo:lo + COL_CHUNK] = x_ref[:, lo:lo + COL_CHUNK] + y
    h_scr[...] = _rms(x2_scr[...], g_ref[...]).astype(BF16)
    _swiglu_into(h_scr, a_scr, win_ref)
    for c in range(D_MODEL // COL_CHUNK):
        lo = c * COL_CHUNK
        y = jnp.dot(a_scr[...], wout_ref[:, lo:lo + COL_CHUNK], preferred_element_type=F32)
        o_ref[:, lo:lo + COL_CHUNK] = x2_scr[:, lo:lo + COL_CHUNK] + 0.5 * y
    if final_norm:
        o_ref[...] = _rms(o_ref[...], fg_ref[...])


def _merge_ffn(x, a, hm, gates, wuf, wum, wo, g, w_in, w_out, fg, final_norm):
    t = x.shape[0]

    def row(w):
        return pl.BlockSpec((ROW_TILE, w), lambda i: (i, 0))

    return pl.pallas_call(
        functools.partial(_merge_ffn_kernel, final_norm=final_norm),
        grid=(t // ROW_TILE,),
        in_specs=[row(D_MODEL), pl.BlockSpec((FOX_WIDTH // LANES, ROW_TILE, LANES), lambda i: (0, i, 0)),
                  row(MLSTM_WIDTH), row(2 * D_MODEL),
                  _resident(wuf.shape), _resident(wum.shape), _resident(wo.shape), _resident((1, D_MODEL)),
                  _resident(w_in.shape), _resident(w_out.shape), _resident((1, D_MODEL))],
        out_specs=row(D_MODEL),
        out_shape=jax.ShapeDtypeStruct((t, D_MODEL), F32),
        scratch_shapes=[pltpu.VMEM((ROW_TILE, D_MODEL), BF16), pltpu.VMEM((ROW_TILE, D_MODEL), F32),
                        pltpu.VMEM((ROW_TILE, D_MODEL), BF16), pltpu.VMEM((ROW_TILE, D_FF), BF16)],
        compiler_params=pltpu.CompilerParams(dimension_semantics=("arbitrary",),
                                             vmem_limit_bytes=VMEM_LIMIT),
        name="merge_ffn",
    )(x, a, hm, gates, wuf, wum, wo, g, w_in, w_out, fg)


def kernel(x, ffn1_norm, ffn1_w_in, ffn1_w_out, mix_norm, w_in, fox_f_bias, mlstm_conv_w, mlstm_conv_b,
           mlstm_i_bias, mlstm_f_bias, w_up_fox, w_up_mlstm, w_out, ffn2_norm, ffn2_w_in, ffn2_w_out,
           final_norm):
    batch, seq, d = x.shape
    depth = ffn1_norm.shape[0]
    assert depth >= 1
    assert d == D_MODEL and seq % ROW_TILE == 0 and seq % ATT_TQ == 0 and seq % MLSTM_CHUNK == 0
    assert ROW_TILE % MLSTM_CHUNK == 0 and mlstm_conv_w.shape[1] == CONV_WIDTH and batch % MLSTM_SEQS == 0
    xf = x.reshape(batch * seq, d)
    fg = final_norm.reshape(1, d)

    o_ff = 3 * FOX_WIDTH
    o_mqk = o_ff + FOX_HEADS
    o_mv = o_mqk + 2 * MLSTM_WIDTH
    o_mi = o_mv + MLSTM_WIDTH
    o_mf = o_mi + MLSTM_HEADS
    o_mo = o_mf + MLSTM_HEADS
    o_g = o_mo + MLSTM_WIDTH

    for l in range(depth):
        w = w_in[l]
        wa = jnp.concatenate([w[:, :FOX_WIDTH] * FOX_HEAD_DIM ** -0.5, w[:, FOX_WIDTH:o_ff]], axis=1).astype(BF16)
        wm = jnp.concatenate([w[:, o_mqk:o_mi], w[:, o_mo:o_g]], axis=1).astype(BF16)
        wg = w[:, o_g:].astype(BF16)
        ws = jnp.concatenate([w[:, o_ff:o_mqk], w[:, o_mi:o_mo]], axis=1).T.astype(BF16)
        bs = jnp.concatenate([fox_f_bias[l], mlstm_i_bias[l], mlstm_f_bias[l]]).reshape(N_SMALL, 1).astype(F32)

        xf = _ffn(xf, ffn1_norm[l].reshape(1, d), ffn1_w_in[l].astype(BF16), ffn1_w_out[l].astype(BF16))
        qkv, mqk, mv, mo, gates, sg = _inproj(
            xf, mix_norm[l].reshape(1, d), wa, wm, wg, ws, bs, mlstm_conv_w[l].astype(F32),
            mlstm_conv_b[l].reshape(1, -1).astype(F32), seq)
        att = _fox_attention(qkv, sg, batch, seq)
        hm = _mlstm(mqk, mv, mo, sg, batch, seq).reshape(batch * seq, MLSTM_WIDTH)
        xf = _merge_ffn(xf, att, hm, gates, w_up_fox[l].astype(BF16), w_up_mlstm[l].astype(BF16),
                        w_out[l].astype(BF16), ffn2_norm[l].reshape(1, d), ffn2_w_in[l].astype(BF16),
                        ffn2_w_out[l].astype(BF16), fg, final_norm=(l == depth - 1))
    return xf.reshape(batch, seq, d)
```

```python
import functools

import jax
import jax.numpy as jnp
from jax import lax
from jax.experimental import pallas as pl
from jax.experimental.pallas import tpu as pltpu

D_MODEL = 1024
FOX_HEADS = 8
FOX_HEAD_DIM = 64
FOX_WIDTH = FOX_HEADS * FOX_HEAD_DIM
MLSTM_HEADS = 4
MLSTM_HEAD_DIM = 128
MLSTM_WIDTH = MLSTM_HEADS * MLSTM_HEAD_DIM
CONV_WIDTH = 4
D_FF = 2816
RMS_EPS = 1e-6
N_SMALL = 16
SG_ROWS = 24
SG_I = FOX_HEADS
SG_LOGF = FOX_HEADS + MLSTM_HEADS
SG_B = N_SMALL + MLSTM_HEADS

V7X_VMEM_BYTES = 64 * 1024 * 1024
LANES = 128
SUBLANES = 8

ROW_TILE = 512
FF_CHUNK = 256
COL_CHUNK = 512
ATT_TQ = 1024
MLSTM_CHUNK = 256
MLSTM_SEQS = 2
VMEM_LIMIT = V7X_VMEM_BYTES * 7 // 8

F32 = jnp.float32
BF16 = jnp.bfloat16


def _rms(x, g):
    ms = jnp.mean(x * x, axis=-1, keepdims=True)
    return x * lax.rsqrt(ms + RMS_EPS) * g


def _sigmoid(x):
    return 1.0 / (1.0 + jnp.exp(-x))


def _log_sigmoid(x):
    return jnp.minimum(x, 0.0) - jnp.log1p(jnp.exp(-jnp.abs(x)))


def _resident(shape):
    nd = len(shape)
    return pl.BlockSpec(shape, lambda *_: (0,) * nd, pipeline_mode=pl.Buffered(1))


def _cumsum_lanes(x, segment=None):
    n = x.shape[-1]
    segment = n if segment is None else segment
    pos = lax.broadcasted_iota(jnp.int32, x.shape, x.ndim - 1) & (segment - 1)
    k = 1
    while k < segment:
        x = x + jnp.where(pos >= k, pltpu.roll(x, k, axis=x.ndim - 1), 0.0)
        k *= 2
    return x


def _swiglu_into(h_scr, a_scr, win_ref):
    for f in range(D_FF // FF_CHUNK):
        lo = f * FF_CHUNK
        h = h_scr[...]
        gate = jnp.dot(h, win_ref[:, lo:lo + FF_CHUNK], preferred_element_type=F32)
        up = jnp.dot(h, win_ref[:, D_FF + lo:D_FF + lo + FF_CHUNK], preferred_element_type=F32)
        a_scr[:, lo:lo + FF_CHUNK] = (gate * _sigmoid(gate) * up).astype(BF16)


def _ffn_kernel(x_ref, g_ref, win_ref, wout_ref, o_ref, h_scr, a_scr):
    h_scr[...] = _rms(x_ref[...], g_ref[...]).astype(BF16)
    _swiglu_into(h_scr, a_scr, win_ref)
    for c in range(D_MODEL // COL_CHUNK):
        lo = c * COL_CHUNK
        y = jnp.dot(a_scr[...], wout_ref[:, lo:lo + COL_CHUNK], preferred_element_type=F32)
        o_ref[:, lo:lo + COL_CHUNK] = x_ref[:, lo:lo + COL_CHUNK] + 0.5 * y


def _ffn(x, g, w_in, w_out):
    t = x.shape[0]
    row = pl.BlockSpec((ROW_TILE, D_MODEL), lambda i: (i, 0))
    return pl.pallas_call(
        _ffn_kernel,
        grid=(t // ROW_TILE,),
        in_specs=[row, _resident((1, D_MODEL)), _resident(w_in.shape), _resident(w_out.shape)],
        out_specs=row,
        out_shape=jax.ShapeDtypeStruct((t, D_MODEL), F32),
        scratch_shapes=[pltpu.VMEM((ROW_TILE, D_MODEL), BF16), pltpu.VMEM((ROW_TILE, D_FF), BF16)],
        compiler_params=pltpu.CompilerParams(dimension_semantics=("arbitrary",),
                                             vmem_limit_bytes=VMEM_LIMIT),
        name="ffn",
    )(x, g, w_in, w_out)


def _inproj_kernel(x_ref, g_ref, wa_ref, wm_ref, wg_ref, ws_ref, bs_ref, cw_ref, cb_ref,
                   qkv_ref, mqk_ref, mv_ref, mo_ref, gates_ref, sg_ref, h_scr, xpad, ccar, *,
                   tiles_per_seq, n_tiles):
    i = pl.program_id(0)
    slot = lax.rem(i, 2)
    prev = 1 - slot

    @pl.when(i == 0)
    def _():
        xpad[1] = jnp.zeros(xpad.shape[1:], F32)
        h_scr[1] = jnp.zeros(h_scr.shape[1:], BF16)
        ccar[...] = jnp.zeros(ccar.shape, F32)

    @pl.when(i % tiles_per_seq == 0)
    def _():
        xpad[slot, 0:SUBLANES, :] = jnp.zeros((SUBLANES, 2 * MLSTM_WIDTH), F32)

    @pl.when(lax.rem(i + tiles_per_seq - 1, tiles_per_seq) == 0)
    def _():
        ccar[...] = jnp.zeros(ccar.shape, F32)

    @pl.when(i % tiles_per_seq != 0)
    def _():
        xpad[slot, 0:SUBLANES, :] = xpad[prev, ROW_TILE:ROW_TILE + SUBLANES, :]

    def proj(h_ref, w_ref, lo):
        return jnp.dot(h_ref[...], w_ref[:, lo:lo + COL_CHUNK], preferred_element_type=F32)

    qscale = MLSTM_HEAD_DIM ** -0.5

    def conv_group(lo, after):
        w = [cw_ref[j:j + 1, lo:lo + LANES] for j in range(CONV_WIDTH)]
        xv = xpad[prev, :, lo:lo + LANES]
        x1 = pltpu.roll(xv, 1, axis=0)
        u = w[3] * xv + w[2] * x1
        v = w[1] * xv + w[0] * x1
        y = (u + pltpu.roll(v, 2, axis=0))[SUBLANES:, :] + (cb_ref[:, lo:lo + LANES] + after)
        y = y * _sigmoid(y)
        if lo < MLSTM_WIDTH:
            y = y * qscale
        mqk_ref[:, lo:lo + LANES] = y.astype(BF16)

    def zero_of(res):
        return jnp.minimum(jnp.abs(res[0:SUBLANES, 0:LANES]), 0.0)[0:1, :]

    def store_cols(o_ref, lo):
        def store(res):
            o_ref[:, lo:lo + COL_CHUNK] = res.astype(BF16)
        return store

    def store_qkv(c):
        def store(res):
            for j in range(COL_CHUNK // LANES):
                qkv_ref[c * (COL_CHUNK // LANES) + j] = res[:, j * LANES:(j + 1) * LANES].astype(BF16)
        return store

    groups = list(range(0, 2 * MLSTM_WIDTH, LANES))
    h_now, h_prev = h_scr.at[slot], h_scr.at[prev]

    lagged = [(store_cols(gates_ref, c * COL_CHUNK), wg_ref, c * COL_CHUNK)
              for c in range(2 * D_MODEL // COL_CHUNK)]
    for idx, (store, w_ref, w_lo) in enumerate(lagged):
        res = proj(h_prev, w_ref, w_lo)
        store(res)
        conv_group(groups[idx], zero_of(res))

    ws = ws_ref[...] + zero_of(res)[:, 0:1].astype(BF16)
    sg = lax.dot_general(ws, h_prev[...], (((1,), (1,)), ((), ())), preferred_element_type=F32) + bs_ref[...]
    r = lax.broadcasted_iota(jnp.int32, sg.shape, 0)
    is_i_gate = (r >= FOX_HEADS) & (r < FOX_HEADS + MLSTM_HEADS)
    sg = jnp.where(is_i_gate, sg, _log_sigmoid(sg))
    c_fox = _cumsum_lanes(sg[0:FOX_HEADS]) + ccar[:, 0:1]
    ccar[...] = jnp.broadcast_to(c_fox[:, ROW_TILE - 1:ROW_TILE], ccar.shape)
    sg_ref[0:FOX_HEADS, :] = -c_fox
    sg_ref[FOX_HEADS:N_SMALL, :] = sg[FOX_HEADS:N_SMALL]
    sg_ref[N_SMALL:SG_ROWS, :] = _cumsum_lanes(sg[FOX_HEADS:N_SMALL], segment=MLSTM_CHUNK)

    h_scr[slot] = _rms(x_ref[...], g_ref[...]).astype(BF16)

    for c in range(2 * MLSTM_WIDTH // COL_CHUNK):
        xpad[slot, SUBLANES:SUBLANES + ROW_TILE, c * COL_CHUNK:(c + 1) * COL_CHUNK] = proj(
            h_now, wm_ref, c * COL_CHUNK)
    others = [(store_cols(mv_ref, 0), wm_ref, 2 * MLSTM_WIDTH), (store_cols(mo_ref, 0), wm_ref, 3 * MLSTM_WIDTH)]
    others += [(store_qkv(c), wa_ref, c * COL_CHUNK) for c in range(3 * FOX_WIDTH // COL_CHUNK)]
    rest = groups[len(lagged):]
    assert len(others) >= len(rest)
    for idx, (store, w_ref, w_lo) in enumerate(others):
        res = proj(h_now, w_ref, w_lo)
        store(res)
        if idx < len(rest):
            conv_group(rest[idx], zero_of(res))


def _inproj(x, g, wa, wm, wg, ws, bs, cw, cb, seq):
    t = x.shape[0]
    nt = t // ROW_TILE

    def row(w):
        return pl.BlockSpec((ROW_TILE, w), lambda i: (jnp.minimum(i, nt - 1), 0))

    qkv_slabs = 3 * FOX_WIDTH // LANES
    out_shape = (
        jax.ShapeDtypeStruct((qkv_slabs, t, LANES), BF16),
        jax.ShapeDtypeStruct((t, 2 * MLSTM_WIDTH), BF16),
        jax.ShapeDtypeStruct((t, MLSTM_WIDTH), BF16),
        jax.ShapeDtypeStruct((t, MLSTM_WIDTH), BF16),
        jax.ShapeDtypeStruct((t, 2 * D_MODEL), BF16),
        jax.ShapeDtypeStruct((SG_ROWS, t), F32),
    )
    return pl.pallas_call(
        functools.partial(_inproj_kernel, tiles_per_seq=seq // ROW_TILE, n_tiles=nt),
        grid=(nt + 1,),
        in_specs=[row(D_MODEL), _resident((1, D_MODEL)), _resident(wa.shape), _resident(wm.shape),
                  _resident(wg.shape), _resident(ws.shape), _resident(bs.shape), _resident(cw.shape),
                  _resident(cb.shape)],
        out_specs=(pl.BlockSpec((qkv_slabs, ROW_TILE, LANES), lambda i: (0, jnp.minimum(i, nt - 1), 0)),
                   pl.BlockSpec((ROW_TILE, 2 * MLSTM_WIDTH), lambda i: (jnp.maximum(i - 1, 0), 0)),
                   row(MLSTM_WIDTH), row(MLSTM_WIDTH),
                   pl.BlockSpec((ROW_TILE, 2 * D_MODEL), lambda i: (jnp.maximum(i - 1, 0), 0)),
                   pl.BlockSpec((SG_ROWS, ROW_TILE), lambda i: (0, jnp.maximum(i - 1, 0)))),
        out_shape=out_shape,
        scratch_shapes=[pltpu.VMEM((2, ROW_TILE, D_MODEL), BF16),
                        pltpu.VMEM((2, ROW_TILE + SUBLANES, 2 * MLSTM_WIDTH), F32),
                        pltpu.VMEM((FOX_HEADS, LANES), F32)],
        compiler_params=pltpu.CompilerParams(dimension_semantics=("arbitrary",),
                                             vmem_limit_bytes=VMEM_LIMIT),
        name="inproj",
    )(x, g, wa, wm, wg, ws, bs, cw, cb)


BIAS_PIECES = 3
VT_ROWS = FOX_HEAD_DIM + 16
MAX_SLAB = 64
ATT_STAGE = 256
PV_CHUNK = 256


def _fox_kernel(q_ref, k_ref, v_ref, sg_ref, o_ref, qa_scr, ka_scr, vt_scr, s_scr, p_scr, *, seq):
    hp = pl.program_id(1)
    tq = ATT_TQ
    tk = ATT_STAGE
    half = FOX_HEAD_DIM

    row = lax.broadcasted_iota(jnp.int32, (SUBLANES, seq), 0)
    blocks = []
    for hh in range(2):
        rem = sg_ref[pl.ds(2 * hp + hh, 1), :]
        blk = jnp.zeros((SUBLANES, seq), F32)
        for p in range(BIAS_PIECES):
            piece = rem.astype(BF16).astype(F32)
            blk = jnp.where(row == p, piece, blk)
            rem = rem - piece
        blocks.append(blk)
    pad = jnp.zeros((half - SUBLANES, seq), F32)
    pieces = jnp.concatenate([blocks[1], pad, blocks[0], pad], axis=0)
    lane = lax.broadcasted_iota(jnp.int32, (tk, LANES), 1)

    def lane_set(lo, hi):
        return jnp.where((lane >= lo) & (lane < hi), 1.0, 0.0).astype(BF16)

    own = (lane_set(0, half), lane_set(half, 2 * half))
    bias_lanes = (lane_set(half, half + BIAS_PIECES), lane_set(0, BIAS_PIECES))
    for c in range(seq // tk):
        sl = slice(c * tk, (c + 1) * tk)
        bias_t = pieces[:, sl].T.astype(BF16)
        k = k_ref[0, sl, :]
        q = q_ref[0, sl, :]
        for hh in range(2):
            ka_scr[hh, sl, :] = k * (1.0 - bias_lanes[hh]) + bias_t * bias_lanes[hh]
            qa_scr[hh, sl, :] = q * own[hh] + bias_lanes[hh]
        vt = v_ref[0, sl, :].astype(F32).T
        vt_scr[0, 0:half, sl] = vt[0:half].astype(BF16)
        vt_scr[1, 0:half, sl] = vt[half:2 * half].astype(BF16)
    ones_rows = jnp.where(lax.broadcasted_iota(jnp.int32, (VT_ROWS - half, seq), 0) == 0, 1.0, 0.0).astype(BF16)
    vt_scr[0, half:VT_ROWS, :] = ones_rows
    vt_scr[1, half:VT_ROWS, :] = ones_rows

    nt = (((1,), (1,)), ((), ()))
    hq = tq // 2
    tri = lax.broadcasted_iota(jnp.int32, (hq, hq), 0) <= lax.broadcasted_iota(jnp.int32, (hq, hq), 1)

    def zero_of(x):
        return jnp.minimum(jnp.abs(x[0:1, 0:1]), 0.0)

    def stage_a(i, hh):
        lo, mid, n = i * tq, i * tq + hq, (i + 1) * tq
        qa = qa_scr[hh, lo:n, :]
        products = []
        if lo > 0:
            past = lax.dot_general(ka_scr[hh, 0:lo, :], qa, nt, preferred_element_type=F32)
            s_scr[hh, 0:lo, :] = past
            products.append(past)
        d1 = lax.dot_general(ka_scr[hh, lo:mid, :], qa, nt, preferred_element_type=F32)
        s_scr[hh, lo:mid, 0:hq] = jnp.where(tri, d1[:, 0:hq], -jnp.inf)
        s_scr[hh, lo:mid, hq:tq] = d1[:, hq:tq]
        products.append(d1)
        d2 = lax.dot_general(ka_scr[hh, mid:n, :], qa[hq:tq], nt, preferred_element_type=F32)
        d2 = jnp.where(tri, d2, -jnp.inf)
        s_scr[hh, mid:n, hq:tq] = d2
        slab = jnp.max(s_scr[hh, 0:mid, :].reshape(mid // MAX_SLAB, MAX_SLAB, tq), axis=0)
        m = jnp.max(slab, axis=0, keepdims=True)
        m_right = jnp.maximum(m[:, hq:tq], jnp.max(d2, axis=0, keepdims=True))
        m = jnp.concatenate([m[:, 0:hq], m_right], axis=1)
        scores = jnp.concatenate(products, axis=0) if len(products) > 1 else products[0]
        return m, m_right, scores

    def stage_b(i, hh, m, m_right, next_scores):
        lo, mid, n = i * tq, i * tq + hq, (i + 1) * tq
        starts = list(range(0, mid, PV_CHUNK))
        acc = None
        for idx, c0 in enumerate(starts):
            c1 = min(c0 + PV_CHUNK, mid)
            mc = m
            if next_scores is not None:
                r = (next_scores.shape[0] * (idx + 1) // (len(starts) + 1)) // SUBLANES * SUBLANES
                mc = m + zero_of(next_scores[r:r + SUBLANES, :])
            p_scr[hh, c0:c1, :] = jnp.exp(s_scr[hh, c0:c1, :] - mc).astype(BF16)
            part = jnp.dot(vt_scr[hh, :, c0:c1], p_scr[hh, c0:c1, :], preferred_element_type=F32)
            acc = part if acc is None else acc + part
        p_scr[hh, mid:n, hq:tq] = jnp.exp(s_scr[hh, mid:n, hq:tq] - m_right).astype(BF16)
        right = jnp.dot(vt_scr[hh, :, mid:n], p_scr[hh, mid:n, hq:tq], preferred_element_type=F32)
        acc = jnp.concatenate([acc[:, 0:hq], acc[:, hq:tq] + right], axis=1)
        return acc[0:half] * (1.0 / acc[half:half + 1])

    units = [(i, hh) for i in range(seq // tq) for hh in range(2)]
    a_now = stage_a(*units[0])
    outs = {}
    for u, (i, hh) in enumerate(units):
        a_next = stage_a(*units[u + 1]) if u + 1 < len(units) else None
        m, m_right, _ = a_now
        outs[hh] = stage_b(i, hh, m, m_right, None if a_next is None else a_next[2])
        a_now = a_next
        if hh == 1:
            o_ref[0, i * tq:(i + 1) * tq, :] = jnp.concatenate([outs[0], outs[1]], axis=0).T.astype(BF16)


def _fox_attention(qkv, sg, batch, seq):
    pairs = FOX_HEADS // 2

    def blk(off):
        return pl.BlockSpec((1, seq, LANES), lambda b, p: (off + p, b, 0))

    return pl.pallas_call(
        functools.partial(_fox_kernel, seq=seq),
        grid=(batch, pairs),
        in_specs=[blk(0), blk(pairs), blk(2 * pairs), pl.BlockSpec((SG_ROWS, seq), lambda b, p: (0, b))],
        out_specs=blk(0),
        out_shape=jax.ShapeDtypeStruct((pairs, batch * seq, LANES), BF16),
        scratch_shapes=[pltpu.VMEM((2, seq, LANES), BF16),
                        pltpu.VMEM((2, seq, LANES), BF16),
                        pltpu.VMEM((2, VT_ROWS, seq), BF16),
                        pltpu.VMEM((2, seq, ATT_TQ), F32),
                        pltpu.VMEM((2, seq, ATT_TQ), BF16)],
        compiler_params=pltpu.CompilerParams(dimension_semantics=("arbitrary", "arbitrary"),
                                             vmem_limit_bytes=VMEM_LIMIT),
        name="fox_attn",
    )(qkv, qkv, qkv, sg)


def _mlstm_kernel(qk_ref, v_ref, mo_ref, sg_ref, o_ref, c_scr, *, seq):
    chains = MLSTM_SEQS * MLSTM_HEADS
    lc = MLSTM_CHUNK
    dk = MLSTM_HEAD_DIM
    t_idx = lax.broadcasted_iota(jnp.int32, (lc, lc), 0)
    s_idx = lax.broadcasted_iota(jnp.int32, (lc, lc), 1)
    causal = s_idx <= t_idx
    c_scr[...] = jnp.zeros(c_scr.shape, F32)

    def chunk(c, carry):
        r0 = pl.multiple_of(c * lc, lc)
        rows = pl.ds(r0, lc)
        new_carry = []
        for ch in range(chains):
            sq, hd = divmod(ch, MLSTM_HEADS)
            n_row, m_prev = carry[ch]
            cmat = c_scr[ch]
            q = qk_ref[sq, rows, hd * dk:(hd + 1) * dk]
            k = qk_ref[sq, rows, MLSTM_WIDTH + hd * dk:MLSTM_WIDTH + (hd + 1) * dk]
            v = v_ref[sq, rows, hd * dk:(hd + 1) * dk]
            gate_cols = pl.ds(sq * seq + r0, lc)
            i_row = sg_ref[SG_I + hd:SG_I + hd + 1, gate_cols]
            lf_row = sg_ref[SG_LOGF + hd:SG_LOGF + hd + 1, gate_cols]
            b_row = sg_ref[SG_B + hd:SG_B + hd + 1, gate_cols]
            b_col = jnp.sum(jnp.where(causal, lf_row, 0.0), axis=-1, keepdims=True)
            g = jnp.sum(lf_row, axis=-1, keepdims=True)
            a_row = i_row - b_row
            dmat = jnp.where(causal, b_col + a_row, -jnp.inf)
            m_inter = b_col + m_prev
            m_t = jnp.maximum(jnp.max(dmat, axis=-1, keepdims=True), m_inter)
            kt = k.astype(F32).T
            s = jnp.dot(q, kt.astype(BF16), preferred_element_type=F32) * jnp.exp(dmat - m_t)
            inter = jnp.exp(m_inter - m_t)
            num = (jnp.dot(s.astype(BF16), v, preferred_element_type=F32)
                   + inter * jnp.dot(q, cmat.astype(BF16), preferred_element_type=F32))
            qn = jnp.sum(q.astype(F32) * n_row, axis=-1, keepdims=True)
            den = jnp.sum(s, axis=-1, keepdims=True) + inter * qn
            h = num * (1.0 / jnp.maximum(jnp.abs(den), jnp.exp(-m_t)))
            gate = _sigmoid(mo_ref[sq, rows, hd * dk:(hd + 1) * dk].astype(F32))
            o_ref[sq, rows, hd * dk:(hd + 1) * dk] = (gate * h).astype(BF16)

            kdec = g + a_row
            m_new = jnp.maximum(g + m_prev, jnp.max(kdec, axis=-1, keepdims=True))
            wk = jnp.exp(kdec - m_new)
            cdec = jnp.exp(g + m_prev - m_new)
            kwt = (kt * wk).astype(BF16)
            c_scr[ch] = cdec * cmat + jnp.dot(kwt, v, preferred_element_type=F32)
            wk8 = jnp.broadcast_to(wk, (SUBLANES, lc)).astype(BF16)
            n_new = cdec * n_row + jnp.dot(wk8, k, preferred_element_type=F32)[0:1, :]
            new_carry.append((n_new, m_new))
        return tuple(new_carry)

    init = tuple((jnp.zeros((1, dk), F32), jnp.zeros((1, 1), F32)) for _ in range(chains))
    lax.fori_loop(0, seq // lc, chunk, init)


def _mlstm(mqk, mv, mo, sg, batch, seq):
    mqk3 = mqk.reshape(batch, seq, 2 * MLSTM_WIDTH)
    mv3 = mv.reshape(batch, seq, MLSTM_WIDTH)
    mo3 = mo.reshape(batch, seq, MLSTM_WIDTH)

    def seq_blk(w):
        return pl.BlockSpec((MLSTM_SEQS, seq, w), lambda b: (b, 0, 0))

    return pl.pallas_call(
        functools.partial(_mlstm_kernel, seq=seq),
        grid=(batch // MLSTM_SEQS,),
        in_specs=[seq_blk(2 * MLSTM_WIDTH), seq_blk(MLSTM_WIDTH), seq_blk(MLSTM_WIDTH),
                  pl.BlockSpec((SG_ROWS, MLSTM_SEQS * seq), lambda b: (0, b))],
        out_specs=seq_blk(MLSTM_WIDTH),
        out_shape=jax.ShapeDtypeStruct((batch, seq, MLSTM_WIDTH), BF16),
        scratch_shapes=[pltpu.VMEM((MLSTM_SEQS * MLSTM_HEADS, MLSTM_HEAD_DIM, MLSTM_HEAD_DIM), F32)],
        compiler_params=pltpu.CompilerParams(dimension_semantics=("arbitrary",),
                                             vmem_limit_bytes=VMEM_LIMIT),
        name="mlstm",
    )(mqk3, mv3, mo3, sg)


def _merge_ffn_kernel(x_ref, a_ref, hm_ref, gates_ref, wuf_ref, wum_ref, wo_ref, g_ref, win_ref, wout_ref,
                      fg_ref, o_ref, mg_scr, x2_scr, h_scr, a_scr, *, final_norm):
    att = jnp.concatenate([a_ref[p] for p in range(FOX_WIDTH // LANES)], axis=1)
    for c in range(D_MODEL // COL_CHUNK):
        lo = c * COL_CHUNK
        a = jnp.dot(att, wuf_ref[:, lo:lo + COL_CHUNK], preferred_element_type=F32)
        bm = jnp.dot(hm_ref[...], wum_ref[:, lo:lo + COL_CHUNK], preferred_element_type=F32)
        ga = gates_ref[:, lo:lo + COL_CHUNK].astype(F32)
        gb = gates_ref[:, D_MODEL + lo:D_MODEL + lo + COL_CHUNK].astype(F32)
        mg_scr[:, lo:lo + COL_CHUNK] = (_sigmoid(ga) * a + _sigmoid(gb) * bm).astype(BF16)
    for c in range(D_MODEL // COL_CHUNK):
        lo = c * COL_CHUNK
        y = jnp.dot(mg_scr[...], wo_ref[:, lo:lo + COL_CHUNK], preferred_element_type=F32)
        x2_scr[:, lo:lo + COL_CHUNK] = x_ref[:, lo:lo + COL_CHUNK] + y
    h_scr[...] = _rms(x2_scr[...], g_ref[...]).astype(BF16)
    _swiglu_into(h_scr, a_scr, win_ref)
    for c in range(D_MODEL // COL_CHUNK):
        lo = c * COL_CHUNK
        y = jnp.dot(a_scr[...], wout_ref[:, lo:lo + COL_CHUNK], preferred_element_type=F32)
        o_ref[:, lo:lo + COL_CHUNK] = x2_scr[:, lo:lo + COL_CHUNK] + 0.5 * y
    if final_norm:
        o_ref[...] = _rms(o_ref[...], fg_ref[...])


def _merge_ffn(x, a, hm, gates, wuf, wum, wo, g, w_in, w_out, fg, final_norm):
    t = x.shape[0]

    def row(w):
        return pl.BlockSpec((ROW_TILE, w), lambda i: (i, 0))

    return pl.pallas_call(
        functools.partial(_merge_ffn_kernel, final_norm=final_norm),
        grid=(t // ROW_TILE,),
        in_specs=[row(D_MODEL), pl.BlockSpec((FOX_WIDTH // LANES, ROW_TILE, LANES), lambda i: (0, i, 0)),
                  row(MLSTM_WIDTH), row(2 * D_MODEL),
                  _resident(wuf.shape), _resident(wum.shape), _resident(wo.shape), _resident((1, D_MODEL)),
                  _resident(w_in.shape), _resident(w_out.shape), _resident((1, D_MODEL))],
        out_specs=row(D_MODEL),
        out_shape=jax.ShapeDtypeStruct((t, D_MODEL), F32),
        scratch_shapes=[pltpu.VMEM((ROW_TILE, D_MODEL), BF16), pltpu.VMEM((ROW_TILE, D_MODEL), F32),
                        pltpu.VMEM((ROW_TILE, D_MODEL), BF16), pltpu.VMEM((ROW_TILE, D_FF), BF16)],
        compiler_params=pltpu.CompilerParams(dimension_semantics=("arbitrary",),
                                             vmem_limit_bytes=VMEM_LIMIT),
        name="merge_ffn",
    )(x, a, hm, gates, wuf, wum, wo, g, w_in, w_out, fg)


def kernel(x, ffn1_norm, ffn1_w_in, ffn1_w_out, mix_norm, w_in, fox_f_bias, mlstm_conv_w, mlstm_conv_b,
           mlstm_i_bias, mlstm_f_bias, w_up_fox, w_up_mlstm, w_out, ffn2_norm, ffn2_w_in, ffn2_w_out,
           final_norm):
    batch, seq, d = x.shape
    depth = ffn1_norm.shape[0]
    assert depth >= 1
    assert d == D_MODEL and seq % ROW_TILE == 0 and seq % ATT_TQ == 0 and seq % MLSTM_CHUNK == 0
    assert ROW_TILE % MLSTM_CHUNK == 0 and mlstm_conv_w.shape[1] == CONV_WIDTH and batch % MLSTM_SEQS == 0
    xf = x.reshape(batch * seq, d)
    fg = final_norm.reshape(1, d)

    o_ff = 3 * FOX_WIDTH
    o_mqk = o_ff + FOX_HEADS
    o_mv = o_mqk + 2 * MLSTM_WIDTH
    o_mi = o_mv + MLSTM_WIDTH
    o_mf = o_mi + MLSTM_HEADS
    o_mo = o_mf + MLSTM_HEADS
    o_g = o_mo + MLSTM_WIDTH

    for l in range(depth):
        w = w_in[l]
        wa = jnp.concatenate([w[:, :FOX_WIDTH] * FOX_HEAD_DIM ** -0.5, w[:, FOX_WIDTH:o_ff]], axis=1).astype(BF16)
        wm = jnp.concatenate([w[:, o_mqk:o_mi], w[:, o_mo:o_g]], axis=1).astype(BF16)
        wg = w[:, o_g:].astype(BF16)
        ws = jnp.concatenate([w[:, o_ff:o_mqk], w[:, o_mi:o_mo]], axis=1).T.astype(BF16)
        bs = jnp.concatenate([fox_f_bias[l], mlstm_i_bias[l], mlstm_f_bias[l]]).reshape(N_SMALL, 1).astype(F32)

        xf = _ffn(xf, ffn1_norm[l].reshape(1, d), ffn1_w_in[l].astype(BF16), ffn1_w_out[l].astype(BF16))
        qkv, mqk, mv, mo, gates, sg = _inproj(
            xf, mix_norm[l].reshape(1, d), wa, wm, wg, ws, bs, mlstm_conv_w[l].astype(F32),
            mlstm_conv_b[l].reshape(1, -1).astype(F32), seq)
        att = _fox_attention(qkv, sg, batch, seq)
        hm = _mlstm(mqk, mv, mo, sg, batch, seq).reshape(batch * seq, MLSTM_WIDTH)
        xf = _merge_ffn(xf, att, hm, gates, w_up_fox[l].astype(BF16), w_up_mlstm[l].astype(BF16),
                        w_out[l].astype(BF16), ffn2_norm[l].reshape(1, d), ffn2_w_in[l].astype(BF16),
                        ffn2_w_out[l].astype(BF16), fg, final_norm=(l == depth - 1))
    return xf.reshape(batch, seq, d)
```
